```python
import math
import jax, jax.numpy as jnp
from jax import lax
import numpy as np

D_MODEL = 1024
BATCH = 2
SEQ = 16384
DEPTH = 1
DEC_BATCH = 1
DEC_SEQ = 16384
PAST_LEN = 128

GRID_W = 64
EPS = 1e-6
SSD_EXPAND = 2
D_INNER = SSD_EXPAND * D_MODEL
SSD_HEAD_DIM = 64
SSD_HEADS = D_INNER // SSD_HEAD_DIM
SSD_GROUPS = 4
SSD_STATE = 128
CONV_W = 5
CHUNK = 128
CONV_DIM = D_INNER + 2 * SSD_GROUPS * SSD_STATE
NA_HEADS = 16
NA_HEAD_DIM = 64
NA_WIDTH = NA_HEADS * NA_HEAD_DIM
WIN_ROWS_MAX = 8
WIN_COLS = 16
PEER_HEADS = 8
PEER_N_KEYS = 128
PEER_N_EXPERTS = PEER_N_KEYS * PEER_N_KEYS
PEER_QUERY_DIM = 256
PEER_TOPK = 16
PEER_BLOCK = 128
OFF_XBC = D_INNER
OFF_DT = OFF_XBC + CONV_DIM
OFF_QKV = OFF_DT + 2 * SSD_HEADS
OFF_GATE = OFF_QKV + 3 * NA_WIDTH
D_IN_PROJ = OFF_GATE + 2 * D_MODEL
MIX_WIDTH = D_INNER + NA_WIDTH

kernel_name = "hybrid_ssd_natten_peer_encoder"


def rms_norm(x, g):
    xf = x.astype(jnp.float32)
    y = xf * lax.rsqrt(jnp.mean(xf * xf, axis=-1, keepdims=True) + EPS)
    return (y * g.astype(jnp.float32)).astype(x.dtype)


def centred_dwconv(u, w, b):
    out = lax.conv_general_dilated(
        u, w[:, None, :].astype(u.dtype), window_strides=(1,),
        padding=[(CONV_W // 2, CONV_W // 2)],
        dimension_numbers=('NWC', 'WIO', 'NWC'),
        feature_group_count=u.shape[-1])
    return out + b.astype(u.dtype)


def ssd_chunked(xs, dt, a, bmat, cmat):
    f32 = jnp.float32
    bsz, L, H, P = xs.shape
    nc = L // CHUNK
    R = H // SSD_GROUPS
    xdt = (xs.astype(f32) * dt[..., None]).reshape(bsz, nc, CHUNK, SSD_GROUPS, R, P)
    bc = bmat.astype(f32).reshape(bsz, nc, CHUNK, SSD_GROUPS, SSD_STATE)
    cc = cmat.astype(f32).reshape(bsz, nc, CHUNK, SSD_GROUPS, SSD_STATE)
    a_cum = jnp.cumsum((dt * a).reshape(bsz, nc, CHUNK, SSD_GROUPS, R), axis=2)
    tril = jnp.tril(jnp.ones((CHUNK, CHUNK), dtype=bool))[:, :, None, None]
    seg = a_cum[:, :, :, None] - a_cum[:, :, None, :]
    decay = jnp.exp(jnp.where(tril, seg, -jnp.inf))
    cb = jnp.einsum('bclgn,bcsgn->bclsg', cc, bc)
    y_diag = jnp.einsum('bclsg,bclsgr,bcsgrp->bclgrp', cb, decay, xdt)
    decay_states = jnp.exp(a_cum[:, :, -1:] - a_cum)
    states = jnp.einsum('bclgn,bclgr,bclgrp->bcgrpn', bc, decay_states, xdt)
    chunk_decay = jnp.exp(a_cum[:, :, -1])

    def step(h, inp):
        s_c, d_c = inp
        return h * d_c[..., None, None] + s_c, h

    h0 = jnp.zeros((bsz, SSD_GROUPS, R, P, SSD_STATE), f32)
    _, prev = lax.scan(step, h0, (jnp.moveaxis(states, 1, 0), jnp.moveaxis(chunk_decay, 1, 0)))
    prev = jnp.moveaxis(prev, 0, 1)
    y_off = jnp.einsum('bclgn,bcgrpn,bclgr->bclgrp', cc, prev, jnp.exp(a_cum))
    return (y_diag + y_off).reshape(bsz, L, H, P)


def ssd_branch(z, xbc, dt_raw, conv_w, conv_b, dt_bias, a_log, d_skip, norm_g):
    f32 = jnp.float32
    bsz, L, _ = z.shape
    xbc = jax.nn.silu(centred_dwconv(xbc, conv_w, conv_b))
    xs, bm, cm = jnp.split(xbc, [D_INNER, D_INNER + SSD_GROUPS * SSD_STATE], axis=-1)
    xs = xs.reshape(bsz, L, SSD_HEADS, SSD_HEAD_DIM)
    bm = bm.reshape(bsz, L, SSD_GROUPS, SSD_STATE)
    cm = cm.reshape(bsz, L, SSD_GROUPS, SSD_STATE)
    dt = jax.nn.softplus(dt_raw.astype(f32).reshape(bsz, L, 2, SSD_HEADS) + dt_bias.astype(f32))
    a = -jnp.exp(a_log.astype(f32))
    flip = lambda t: jnp.flip(t, axis=1)
    y_fwd = ssd_chunked(xs, dt[:, :, 0], a[0], bm, cm)
    y_bwd = flip(ssd_chunked(flip(xs), flip(dt[:, :, 1]), a[1], flip(bm), flip(cm)))
    y = y_fwd + y_bwd + d_skip.astype(f32)[:, None] * xs.astype(f32)
    y = y.reshape(bsz, L, D_INNER) * jax.nn.silu(z.astype(f32))
    return rms_norm(y, norm_g).astype(z.dtype)


def na_branch(q, k, v, q_norm_g, k_norm_g, rpb):
    bsz, L, _ = q.shape
    rows = L // GRID_W
    win_r = min(WIN_ROWS_MAX, rows)
    heads = lambda t: t.reshape(bsz, rows, GRID_W, NA_HEADS, NA_HEAD_DIM)
    q = rms_norm(heads(q), q_norm_g) * (NA_HEAD_DIM ** -0.5)
    k = rms_norm(heads(k), k_norm_g)
    v = heads(v)
    cols = np.arange(GRID_W)
    c0 = np.clip(cols - WIN_COLS // 2, 0, GRID_W - WIN_COLS)
    col_idx = c0[:, None] + np.arange(WIN_COLS)[None, :]
    col_off = col_idx - cols[:, None] + (WIN_COLS - 1)
    rpb_cols = jnp.transpose(rpb[:, :, col_off], (0, 2, 1, 3))

    def one_row(r):
        r0 = jnp.clip(r - win_r // 2, 0, rows - win_r)
        kr = lax.dynamic_slice_in_dim(k, r0, win_r, axis=1)
        vr = lax.dynamic_slice_in_dim(v, r0, win_r, axis=1)
        kw = kr[:, :, col_idx]
        vw = vr[:, :, col_idx]
        qr = lax.dynamic_index_in_dim(q, r, axis=1, keepdims=False)
        s = jnp.einsum('bchd,bicjhd->bhcij', qr, kw, preferred_element_type=jnp.float32)
        row_off = r0 + jnp.arange(win_r) - r + (WIN_ROWS_MAX - 1)
        s = s + jnp.take(rpb_cols, row_off, axis=2).astype(jnp.float32)[None]
        p = jax.nn.softmax(s, axis=(-2, -1))
        return jnp.einsum('bhcij,bicjhd->bchd', p.astype(v.dtype), vw)

    out = lax.map(one_row, jnp.arange(rows))
    return jnp.transpose(out, (1, 0, 2, 3, 4)).reshape(bsz, L, NA_WIDTH)


def peer_ffn(h, w_pq, sub_keys, expert_u, expert_v):
    f32 = jnp.float32
    bsz, L, D = h.shape
    t = h.reshape(-1, D)
    nt = t.shape[0]
    q = (t @ w_pq).reshape(nt, PEER_HEADS, 2, PEER_QUERY_DIM // 2)
    s = jnp.einsum('thkd,knd->thkn', q, sub_keys).astype(f32)
    top_s, top_i = lax.top_k(s, PEER_TOPK)
    cand = top_s[:, :, 0, :, None] + top_s[:, :, 1, None, :]
    cand_idx = top_i[:, :, 0, :, None] * PEER_N_KEYS + top_i[:, :, 1, None, :]
    best_s, best_pos = lax.top_k(cand.reshape(nt, PEER_HEADS, -1), PEER_TOPK)
    idx = jnp.take_along_axis(cand_idx.reshape(nt, PEER_HEADS, -1), best_pos, axis=-1)
    g = jax.nn.softmax(best_s, axis=-1)
    nb = nt // PEER_BLOCK

    def block(args):
        tb, ib, gb = args
        u = expert_u[ib]
        vv = expert_v[ib]
        act = jax.nn.gelu(jnp.einsum('td,thkd->thk', tb, u).astype(f32), approximate=False)
        return jnp.einsum('thk,thkd->td', (gb * act).astype(vv.dtype), vv)

    out = lax.map(block, (t.reshape(nb, PEER_BLOCK, D),
                          idx.reshape(nb, PEER_BLOCK, PEER_HEADS, PEER_TOPK),
                          g.reshape(nb, PEER_BLOCK, PEER_HEADS, PEER_TOPK)))
    return out.reshape(bsz, L, D).astype(h.dtype)


def encoder_layer(x, g_mix, w_in, conv_w, conv_b, dt_bias, a_log, d_skip, ssd_norm_g,
                  q_norm_g, k_norm_g, rpb, w_out, g_ffn, w_pq, sub_keys, expert_u, expert_v):
    h = rms_norm(x, g_mix)
    proj = h @ w_in
    z, xbc, dt_raw, qkv, gates = jnp.split(proj, [OFF_XBC, OFF_DT, OFF_QKV, OFF_GATE], axis=-1)
    q, k, v = jnp.split(qkv, 3, axis=-1)
    y_ssd = ssd_branch(z, xbc, dt_raw, conv_w, conv_b, dt_bias, a_log, d_skip, ssd_norm_g)
    y_na = na_branch(q, k, v, q_norm_g, k_norm_g, rpb)
    gate_ssd, gate_na = jnp.split(jax.nn.sigmoid(gates.astype(jnp.float32)), 2, axis=-1)
    br_ssd = y_ssd @ w_out[:D_INNER]
    br_na = y_na @ w_out[D_INNER:]
    x = x + (gate_ssd * br_ssd + gate_na * br_na).astype(x.dtype)
    x = x + peer_ffn(rms_norm(x, g_ffn), w_pq, sub_keys, expert_u, expert_v)
    return x


def setup_inputs(seed: int = 0) -> dict:
    key = jax.random.key(seed)
    ks = jax.random.split(key, 20)
    f32 = jnp.float32
    nrm = lambda k, shape, scale: scale * jax.random.normal(k, shape, f32)
    dt0 = jnp.exp(jax.random.uniform(ks[6], (DEPTH, 2, SSD_HEADS), f32, math.log(1e-3), math.log(1e-1)))
    return {
        "x_prompt": jax.random.normal(ks[0], (BATCH, SEQ, D_MODEL), f32),
        "x_sample": jax.random.normal(ks[1], (DEC_BATCH, DEC_SEQ, D_MODEL), f32),
        "g_mix": 1.0 + nrm(ks[2], (DEPTH, D_MODEL), 0.02),
        "w_in": nrm(ks[3], (DEPTH, D_MODEL, D_IN_PROJ), D_MODEL ** -0.5),
        "conv_w": nrm(ks[4], (DEPTH, CONV_W, CONV_DIM), CONV_W ** -0.5),
        "conv_b": nrm(ks[5], (DEPTH, CONV_DIM), 0.01),
        "dt_bias": dt0 + jnp.log(-jnp.expm1(-dt0)),
        "a_log": jnp.log(jax.random.uniform(ks[7], (DEPTH, 2, SSD_HEADS), f32, 1.0, 16.0)),
        "d_skip": 1.0 + nrm(ks[8], (DEPTH, SSD_HEADS), 0.02),
        "ssd_norm_g": 1.0 + nrm(ks[9], (DEPTH, D_INNER), 0.02),
        "q_norm_g": 1.0 + nrm(ks[10], (DEPTH, NA_HEAD_DIM), 0.02),
        "k_norm_g": 1.0 + nrm(ks[11], (DEPTH, NA_HEAD_DIM), 0.02),
        "rpb": nrm(ks[12], (DEPTH, NA_HEADS, 2 * WIN_ROWS_MAX - 1, 2 * WIN_COLS - 1), 0.02),
        "w_out": nrm(ks[13], (DEPTH, MIX_WIDTH, D_MODEL), MIX_WIDTH ** -0.5),
        "g_ffn": 1.0 + nrm(ks[14], (DEPTH, D_MODEL), 0.02),
        "w_pq": nrm(ks[15], (DEPTH, D_MODEL, PEER_HEADS * PEER_QUERY_DIM), D_MODEL ** -0.5),
        "sub_keys": nrm(ks[16], (DEPTH, 2, PEER_N_KEYS, PEER_QUERY_DIM // 2), (PEER_QUERY_DIM // 2) ** -0.5),
        "expert_u": nrm(ks[17], (DEPTH, PEER_N_EXPERTS, D_MODEL), D_MODEL ** -0.5),
        "expert_v": nrm(ks[18], (DEPTH, PEER_N_EXPERTS, D_MODEL), (PEER_HEADS * PEER_TOPK) ** -0.5),
    }


def reference(x_prompt, x_sample, g_mix, w_in, conv_w, conv_b, dt_bias, a_log, d_skip, ssd_norm_g,
              q_norm_g, k_norm_g, rpb, w_out, g_ffn, w_pq, sub_keys, expert_u, expert_v):
    y_prompt = x_prompt
    y_sample = x_sample
    for i in range(DEPTH):
        layer_params = (g_mix[i], w_in[i], conv_w[i], conv_b[i], dt_bias[i], a_log[i], d_skip[i],
                        ssd_norm_g[i], q_norm_g[i], k_norm_g[i], rpb[i], w_out[i], g_ffn[i],
                        w_pq[i], sub_keys[i], expert_u[i], expert_v[i])
        y_prompt = encoder_layer(y_prompt, *layer_params)
        y_sample = encoder_layer(y_sample, *layer_params)
    return (y_prompt, y_sample)
```

```python
import functools

import jax
import jax.numpy as jnp
from jax import lax
from jax.experimental import pallas as pl
from jax.experimental.pallas import tpu as pltpu

F32 = jnp.float32
BF16 = jnp.bfloat16

D_MODEL = 1024
GRID_W = 64
EPS = 1e-6
D_INNER = 2048
SSD_HEAD_DIM = 64
SSD_HEADS = 32
SSD_GROUPS = 4
SSD_STATE = 128
CONV_W = 5
CHUNK = 128
CONV_DIM = D_INNER + 2 * SSD_GROUPS * SSD_STATE
NA_HEADS = 16
NA_HEAD_DIM = 64
NA_WIDTH = 1024
WIN_ROWS = 8
WIN_COLS = 16
PEER_HEADS = 8
PEER_N_KEYS = 128
PEER_TOPK = 16
PEER_QUERY_DIM = 256
OFF_XBC = D_INNER
OFF_DT = OFF_XBC + CONV_DIM
OFF_QKV = OFF_DT + 2 * SSD_HEADS
OFF_GATE = OFF_QKV + 3 * NA_WIDTH

LANES = 128
PROJ_COLS = CONV_DIM + 3 * NA_WIDTH + D_INNER + 2 * D_MODEL + LANES
COL_QKV = CONV_DIM
COL_Z = COL_QKV + 3 * NA_WIDTH
COL_GATE = COL_Z + D_INNER
COL_DT = COL_GATE + 2 * D_MODEL

VMEM_LIMIT = 56 * 1024 * 1024


def _cparams(sem):
    return pltpu.CompilerParams(dimension_semantics=sem, vmem_limit_bytes=VMEM_LIMIT)


def _rms(x, g):
    return x * lax.rsqrt(jnp.mean(x * x, axis=-1, keepdims=True) + EPS) * g


INPROJ_TM = 1024
INPROJ_TN = 1152


def _inproj_kernel(x_ref, g_ref, w_ref, o_ref, dt_ref, h_scr):
    j = pl.program_id(1)

    @pl.when(j == 0)
    def _():
        h_scr[...] = _rms(x_ref[...], g_ref[...]).astype(BF16)

    acc = jnp.dot(h_scr[...], w_ref[...], preferred_element_type=F32)
    o_ref[...] = acc.astype(BF16)

    @pl.when(j == pl.num_programs(1) - 1)
    def _():
        dt_ref[...] = acc[:, INPROJ_TN - LANES:]


def _inproj(x2d, g_mix, w_cat):
    t = x2d.shape[0]
    tm, tn = INPROJ_TM, INPROJ_TN
    assert t % tm == 0 and PROJ_COLS % tn == 0 and COL_DT == PROJ_COLS - LANES
    return pl.pallas_call(
        _inproj_kernel,
        grid=(t // tm, PROJ_COLS // tn),
        in_specs=[
            pl.BlockSpec((tm, D_MODEL), lambda i, j: (i, 0)),
            pl.BlockSpec((1, D_MODEL), lambda i, j: (0, 0)),
            pl.BlockSpec((D_MODEL, tn), lambda i, j: (0, j)),
        ],
        out_specs=[
            pl.BlockSpec((tm, tn), lambda i, j: (i, j)),
            pl.BlockSpec((tm, LANES), lambda i, j: (i, 0)),
        ],
        out_shape=[
            jax.ShapeDtypeStruct((t, PROJ_COLS), BF16),
            jax.ShapeDtypeStruct((t, LANES), F32),
        ],
        scratch_shapes=[pltpu.VMEM((tm, D_MODEL), BF16)],
        compiler_params=_cparams(("parallel", "arbitrary")),
        name="inproj",
    )(x2d, g_mix.reshape(1, D_MODEL), w_cat)


CONV_TL = 512
CONV_CB = 1024
HALO = 16
PAD = 8


def _conv_kernel(cur_ref, prev_ref, next_ref, w_ref, b_ref, o_ref, ext):
    i = pl.program_id(1)
    tl = cur_ref.shape[0]
    prev = prev_ref[...].astype(F32)[HALO - PAD:, :]
    nxt = next_ref[...].astype(F32)[:PAD, :]
    ext[0:PAD, :] = jnp.where(i > 0, prev, 0.0)
    ext[PAD:PAD + tl, :] = cur_ref[...].astype(F32)
    ext[PAD + tl:, :] = jnp.where(i < pl.num_programs(1) - 1, nxt, 0.0)
    acc = jnp.broadcast_to(b_ref[...], o_ref.shape)
    for k in range(CONV_W):
        acc = acc + w_ref[k:k + 1, :] * ext[pl.ds(PAD - CONV_W // 2 + k, tl), :]
    o_ref[...] = (acc * jax.nn.sigmoid(acc)).astype(BF16)


def _conv(proj, conv_w, conv_b, nseq, seqlen):
    t = proj.shape[0]
    tl, cb = CONV_TL, CONV_CB
    nl = seqlen // tl
    assert seqlen % tl == 0 and CONV_DIM % cb == 0
    return pl.pallas_call(
        _conv_kernel,
        grid=(nseq, nl, CONV_DIM // cb),
        in_specs=[
            pl.BlockSpec((tl, cb), lambda s, i, c: (s * nl + i, c)),
            pl.BlockSpec((HALO, cb), lambda s, i, c: (jnp.maximum((s * nl + i) * (tl // HALO) - 1, 0), c)),
            pl.BlockSpec((HALO, cb),
                         lambda s, i, c: (jnp.minimum((s * nl + i + 1) * (tl // HALO), t // HALO - 1), c)),
            pl.BlockSpec((CONV_W, cb), lambda s, i, c: (0, c)),
            pl.BlockSpec((1, cb), lambda s, i, c: (0, c)),
        ],
        out_specs=pl.BlockSpec((tl, cb), lambda s, i, c: (s * nl + i, c)),
        out_shape=jax.ShapeDtypeStruct((t, CONV_DIM), BF16),
        scratch_shapes=[pltpu.VMEM((tl + 2 * PAD, cb), F32)],
        compiler_params=_cparams(("parallel", "parallel", "parallel")),
        name="conv",
    )(proj, proj, proj, conv_w, conv_b.reshape(1, CONV_DIM))


def _cumsum_rows(x):
    row = lax.broadcasted_iota(jnp.int32, x.shape, 0)
    d = 1
    while d < x.shape[0]:
        x = x + jnp.where(row >= d, pltpu.roll(x, d, 0), 0.0)
        d *= 2
    return x


def _ssd_direction(reverse, xs_ref, b_ref, c_ref, dt_ref, dtb_ref, alog_ref, y_ref, st):
    row = lax.broadcasted_iota(jnp.int32, (CHUNK, CHUNK), 0)
    col = lax.broadcasted_iota(jnp.int32, (CHUNK, CHUNK), 1)
    lane1 = lax.broadcasted_iota(jnp.int32, (1, LANES), 1)
    dt_pre = dt_ref[...] + dtb_ref[...]
    dt = jnp.maximum(dt_pre, 0.0) + jnp.log1p(jnp.exp(-jnp.abs(dt_pre)))
    dta = dt * (-jnp.exp(alog_ref[...]))
    cs = _cumsum_rows(dta)
    total = cs[CHUNK - 1:CHUNK, :]
    if reverse:
        cum = total - cs + dta
        wexp = cs - dta
        tri = col >= row
    else:
        cum = cs
        wexp = total - cs
        tri = row >= col
    w = jnp.exp(wexp) * dt
    cum_t = cum.T
    dt_t = dt.T
    w_t = w.T
    dec = jnp.exp(total)
    base = SSD_HEADS if reverse else 0
    pairs_per_group = SSD_HEADS // SSD_GROUPS // 2
    for g in range(SSD_GROUPS):
        bg = b_ref[:, g * SSD_STATE:(g + 1) * SSD_STATE]
        cg = c_ref[:, g * SSD_STATE:(g + 1) * SSD_STATE]
        gmat = lax.dot_general(cg, bg, (((1,), (1,)), ((), ())), preferred_element_type=F32)
        bg_t = bg.astype(F32).T
        cg32 = cg.astype(F32)
        for hp in range(pairs_per_group):
            j = g * pairs_per_group + hp
            sl = slice(j * LANES, (j + 1) * LANES)
            xs_pair = xs_ref[:, sl].astype(F32)
            st_pair = st[:, sl]
            y_acc = jnp.zeros((CHUNK, LANES), F32)
            s_acc = jnp.zeros((SSD_STATE, LANES), F32)
            dec_pair = jnp.zeros((1, LANES), F32)
            for e in range(2):
                ln = base + 2 * j + e
                in_head = (col // SSD_HEAD_DIM) == e
                colb = jnp.broadcast_to(cum[:, ln:ln + 1], (CHUNK, CHUNK))
                decay = jnp.where(tri, jnp.exp(colb - cum_t[ln:ln + 1, :]), 0.0)
                m = gmat * decay * dt_t[ln:ln + 1, :]
                ce = cg32 * jnp.exp(colb)
                lhs = jnp.concatenate([m, ce], axis=1).astype(BF16)
                xs_m = jnp.where(in_head, xs_pair, 0.0).astype(BF16)
                st_m = jnp.where(in_head, st_pair, 0.0).astype(BF16)
                rhs = jnp.concatenate([xs_m, st_m], axis=0)
                y_acc = y_acc + jnp.dot(lhs, rhs, preferred_element_type=F32)
                btw = (bg_t * w_t[ln:ln + 1, :]).astype(BF16)
                s_acc = s_acc + jnp.dot(btw, xs_m, preferred_element_type=F32)
                dec_pair = jnp.where((lane1 // SSD_HEAD_DIM) == e,
                                     jnp.broadcast_to(dec[:, ln:ln + 1], (1, LANES)), dec_pair)
            y_ref[:, sl] = y_acc.astype(BF16)
            st[:, sl] = st_pair * dec_pair + s_acc


def _ssd_kernel(xs_f, b_f, c_f, dt_f, xs_b, b_b, c_b, dt_b, dtb_ref, alog_ref, yf_ref, yb_ref, st_f, st_b):
    @pl.when(pl.program_id(1) == 0)
    def _():
        st_f[...] = jnp.zeros_like(st_f)
        st_b[...] = jnp.zeros_like(st_b)

    _ssd_direction(False, xs_f, b_f, c_f, dt_f, dtb_ref, alog_ref, yf_ref, st_f)
    _ssd_direction(True, xs_b, b_b, c_b, dt_b, dtb_ref, alog_ref, yb_ref, st_b)


def _ssd(xbc, dt_raw, dt_bias, a_log, nseq, seqlen):
    t = xbc.shape[0]
    nc = seqlen // CHUNK
    bn = SSD_GROUPS * SSD_STATE
    fwd = lambda s, c: s * nc + c
    bwd = lambda s, c: s * nc + (nc - 1 - c)

    def specs(rb):
        return [
            pl.BlockSpec((CHUNK, D_INNER), lambda s, c: (rb(s, c), 0)),
            pl.BlockSpec((CHUNK, bn), lambda s, c: (rb(s, c), D_INNER // bn)),
            pl.BlockSpec((CHUNK, bn), lambda s, c: (rb(s, c), D_INNER // bn + 1)),
        ]

    dt_spec = lambda rb: pl.BlockSpec((CHUNK, LANES), lambda s, c: (rb(s, c), 0))
    row_spec = pl.BlockSpec((1, LANES), lambda s, c: (0, 0))
    pad = jnp.zeros((LANES - 2 * SSD_HEADS,), F32)
    dtb = jnp.concatenate([dt_bias.reshape(-1), pad]).reshape(1, LANES)
    alog = jnp.concatenate([a_log.reshape(-1), pad]).reshape(1, LANES)
    return pl.pallas_call(
        _ssd_kernel,
        grid=(nseq, nc),
        in_specs=specs(fwd) + [dt_spec(fwd)] + specs(bwd) + [dt_spec(bwd)] + [row_spec, row_spec],
        out_specs=[
            pl.BlockSpec((CHUNK, D_INNER), lambda s, c: (fwd(s, c), 0)),
            pl.BlockSpec((CHUNK, D_INNER), lambda s, c: (bwd(s, c), 0)),
        ],
        out_shape=[jax.ShapeDtypeStruct((t, D_INNER), BF16)] * 2,
        scratch_shapes=[pltpu.VMEM((SSD_STATE, D_INNER), F32)] * 2,
        compiler_params=_cparams(("parallel", "arbitrary")),
        name="ssd",
    )(xbc, xbc, xbc, dt_raw, xbc, xbc, xbc, dt_raw, dtb, alog)


QKN_TM = 512


def _qknorm_kernel(q_ref, k_ref, gq_ref, gk_ref, qo_ref, ko_ref):
    r = lax.broadcasted_iota(jnp.int32, (LANES, LANES), 0) // NA_HEAD_DIM
    c = lax.broadcasted_iota(jnp.int32, (LANES, LANES), 1) // NA_HEAD_DIM
    same_head = jnp.where(r == c, 1.0, 0.0).astype(BF16)
    for src, g_ref, dst, scale in ((q_ref, gq_ref, qo_ref, NA_HEAD_DIM ** -0.5), (k_ref, gk_ref, ko_ref, None)):
        for p in range(NA_WIDTH // LANES):
            sl = slice(p * LANES, (p + 1) * LANES)
            x = src[:, sl].astype(F32)
            sq = x * x
            hi = sq.astype(BF16)
            lo = (sq - hi.astype(F32)).astype(BF16)
            ss = (jnp.dot(hi, same_head, preferred_element_type=F32)
                  + jnp.dot(lo, same_head, preferred_element_type=F32))
            y = x * lax.rsqrt(ss / NA_HEAD_DIM + EPS) * g_ref[...]
            if scale is not None:
                y = y * scale
            dst[:, sl] = y.astype(BF16)


def _qknorm(proj, q_norm_g, k_norm_g):
    t = proj.shape[0]
    tm = QKN_TM
    reps = LANES // NA_HEAD_DIM
    gq = jnp.tile(q_norm_g, reps).reshape(1, LANES)
    gk = jnp.tile(k_norm_g, reps).reshape(1, LANES)
    qcol = COL_QKV // NA_WIDTH
    return pl.pallas_call(
        _qknorm_kernel,
        grid=(t // tm,),
        in_specs=[
            pl.BlockSpec((tm, NA_WIDTH), lambda i: (i, qcol)),
            pl.BlockSpec((tm, NA_WIDTH), lambda i: (i, qcol + 1)),
            pl.BlockSpec((1, LANES), lambda i: (0, 0)),
            pl.BlockSpec((1, LANES), lambda i: (0, 0)),
        ],
        out_specs=[pl.BlockSpec((tm, NA_WIDTH), lambda i: (i, 0))] * 2,
        out_shape=[jax.ShapeDtypeStruct((t, NA_WIDTH), BF16)] * 2,
        compiler_params=_cparams(("parallel",)),
        name="qknorm",
    )(proj, proj, gq, gk)


def _natten_kernel(*refs):
    q_ref = refs[0]
    k_refs = refs[1:1 + WIN_ROWS]
    v_refs = refs[1 + WIN_ROWS:1 + 2 * WIN_ROWS]
    bias_ref = refs[1 + 2 * WIN_ROWS]
    o_ref = refs[2 + 2 * WIN_ROWS]
    lane = lax.broadcasted_iota(jnp.int32, (GRID_W, LANES), 1)
    for p in range(NA_WIDTH // LANES):
        sl = slice(p * LANES, (p + 1) * LANES)
        qp = q_ref[:, sl].astype(F32)
        kp = jnp.concatenate([r[:, sl] for r in k_refs], axis=0)
        vp = jnp.concatenate([r[:, sl] for r in v_refs], axis=0)
        outs = []
        for e in range(LANES // NA_HEAD_DIM):
            qe = jnp.where((lane // NA_HEAD_DIM) == e, qp, 0.0).astype(BF16)
            s = lax.dot_general(qe, kp, (((1,), (1,)), ((), ())), preferred_element_type=F32)
            s = s + bias_ref[0, p * (LANES // NA_HEAD_DIM) + e]
            m = jnp.max(s, axis=-1, keepdims=True)
            pe = jnp.exp(s - m)
            denom = jnp.sum(pe, axis=-1, keepdims=True)
            o = jnp.dot(pe.astype(BF16), vp, preferred_element_type=F32)
            outs.append(o / denom)
        o_ref[:, sl] = jnp.where(lane < NA_HEAD_DIM, outs[0], outs[1]).astype(BF16)


def _na_bias_table(rpb, rows):
    cols = jnp.arange(GRID_W)
    c0 = jnp.clip(cols - WIN_COLS // 2, 0, GRID_W - WIN_COLS)
    j = jnp.arange(GRID_W)
    valid = (j[None, :] >= c0[:, None]) & (j[None, :] < c0[:, None] + WIN_COLS)
    col_off = jnp.clip(j[None, :] - cols[:, None] + (WIN_COLS - 1), 0, 2 * WIN_COLS - 2)
    delta = jnp.arange(WIN_ROWS)
    i = jnp.arange(WIN_ROWS)
    row_off = i[None, :] - delta[:, None] + (WIN_ROWS - 1)
    tbl = rpb[:, row_off[:, :, None, None], col_off[None, None, :, :]]
    tbl = jnp.where(valid[None, None, None], tbl, -jnp.inf)
    tbl = jnp.transpose(tbl, (1, 0, 3, 2, 4))
    return tbl.reshape(WIN_ROWS, NA_HEADS, GRID_W, WIN_ROWS * GRID_W).astype(F32)


def _natten(qn, kn, proj, rpb, nseq, seqlen):
    t = qn.shape[0]
    rows = seqlen // GRID_W
    assert rows >= WIN_ROWS
    r0 = lambda r: jnp.clip(r - WIN_ROWS // 2, 0, rows - WIN_ROWS)
    vcol = COL_QKV // NA_WIDTH + 2
    tbl = _na_bias_table(rpb, rows)

    def win_spec(i, colblk):
        return pl.BlockSpec((GRID_W, NA_WIDTH), lambda s, r: (s * rows + r0(r) + i, colblk))

    return pl.pallas_call(
        _natten_kernel,
        grid=(nseq, rows),
        in_specs=([pl.BlockSpec((GRID_W, NA_WIDTH), lambda s, r: (s * rows + r, 0))]
                  + [win_spec(i, 0) for i in range(WIN_ROWS)]
                  + [win_spec(i, vcol) for i in range(WIN_ROWS)]
                  + [pl.BlockSpec((1, NA_HEADS, GRID_W, WIN_ROWS * GRID_W),
                                  lambda s, r: (r - r0(r), 0, 0, 0))]),
        out_specs=pl.BlockSpec((GRID_W, NA_WIDTH), lambda s, r: (s * rows + r, 0)),
        out_shape=jax.ShapeDtypeStruct((t, NA_WIDTH), BF16),
        compiler_params=_cparams(("parallel", "arbitrary")),
        name="natten",
    )(qn, *([kn] * WIN_ROWS), *([proj] * WIN_ROWS), tbl)


OUTPROJ_TM = 512


def _outproj_kernel(yf_ref, yb_ref, xs_ref, z_ref, g1_ref, g2_ref, yna_ref, x_ref, w1_ref, w2_ref,
                    dsk_ref, ng_ref, gffn_ref, x1_ref, t_ref):
    y = yf_ref[...].astype(F32) + yb_ref[...].astype(F32) + dsk_ref[...] * xs_ref[...].astype(F32)
    y = y * jax.nn.silu(z_ref[...].astype(F32))
    yn = _rms(y, ng_ref[...]).astype(BF16)
    br_ssd = jnp.dot(yn, w1_ref[...], preferred_element_type=F32)
    br_na = jnp.dot(yna_ref[...], w2_ref[...], preferred_element_type=F32)
    mix = (jax.nn.sigmoid(g1_ref[...].astype(F32)) * br_ssd
           + jax.nn.sigmoid(g2_ref[...].astype(F32)) * br_na)
    x1 = x_ref[...] + mix
    x1_ref[...] = x1
    t_ref[...] = _rms(x1, gffn_ref[...]).astype(BF16)


def _outproj(yf, yb, xbc, proj, yna, x2d, w_out, d_skip, ssd_norm_g, g_ffn):
    t = x2d.shape[0]
    tm = OUTPROJ_TM
    tok = lambda w, cb: pl.BlockSpec((tm, w), lambda i: (i, cb))
    row = lambda w: pl.BlockSpec((1, w), lambda i: (0, 0))
    dsk = jnp.repeat(d_skip, SSD_HEAD_DIM).reshape(1, D_INNER)
    return pl.pallas_call(
        _outproj_kernel,
        grid=(t // tm,),
        in_specs=[
            tok(D_INNER, 0), tok(D_INNER, 0), tok(D_INNER, 0),
            tok(D_INNER, COL_Z // D_INNER),
            tok(D_MODEL, COL_GATE // D_MODEL), tok(D_MODEL, COL_GATE // D_MODEL + 1),
            tok(NA_WIDTH, 0), tok(D_MODEL, 0),
            pl.BlockSpec((D_INNER, D_MODEL), lambda i: (0, 0)),
            pl.BlockSpec((NA_WIDTH, D_MODEL), lambda i: (D_INNER // NA_WIDTH, 0)),
            row(D_INNER), row(D_INNER), row(D_MODEL),
        ],
        out_specs=[tok(D_MODEL, 0), tok(D_MODEL, 0)],
        out_shape=[jax.ShapeDtypeStruct((t, D_MODEL), F32), jax.ShapeDtypeStruct((t, D_MODEL), BF16)],
        compiler_params=_cparams(("parallel",)),
        name="outproj",
    )(yf, yb, xbc, proj, proj, proj, yna, x2d, w_out, w_out, dsk,
      ssd_norm_g.reshape(1, D_INNER), g_ffn.reshape(1, D_MODEL))


ROUTE_TR = 256
NEG_INF = float("-inf")


def _top_values(s, k, out_ref=None):
    mx = None
    for r in range(k):
        mx = jnp.max(s, axis=0, keepdims=True)
        if out_ref is not None:
            out_ref[r:r + 1, :] = mx
        s = jnp.where(s == mx, NEG_INF, s)
    return mx


def _route_kernel(t_ref, wpq_ref, sk_ref, s1_ref, s2_ref, a_ref, b_ref, tau_ref, top1, top2):
    half = PEER_QUERY_DIM // 2
    sub = 8
    rank = lax.broadcasted_iota(jnp.int32, (sub, t_ref.shape[0]), 0)
    q = jnp.dot(t_ref[...], wpq_ref[...], preferred_element_type=F32).astype(BF16)
    for h in range(PEER_HEADS):
        s = []
        for side in range(2):
            qb = q[:, (2 * h + side) * half:(2 * h + side + 1) * half]
            s.append(lax.dot_general(sk_ref[side], qb, (((1,), (1,)), ((), ())),
                                     preferred_element_type=F32))
        _top_values(s[0], PEER_TOPK, top1)
        _top_values(s[1], PEER_TOPK, top2)
        tiles = [top1[0:1, :] + top2[0:sub, :], top1[0:1, :] + top2[sub:, :]]
        for r1 in range(1, sub):
            tile = top1[r1:r1 + 1, :] + top2[0:sub, :]
            tiles.append(jnp.where(rank < PEER_TOPK // (r1 + 1), tile, NEG_INF))
        tiles.append(top1[sub:, :] + top2[0:1, :])
        cand = jnp.concatenate(tiles, axis=0)
        tau = _top_values(cand, PEER_TOPK)
        m1 = top1[0:1, :]
        m2 = top2[0:1, :]
        z = jnp.sum(jnp.where(cand >= tau, jnp.exp(cand - (m1 + m2)), 0.0), axis=0, keepdims=True)
        s1_ref[h] = s[0]
        s2_ref[h] = s[1]
        a_ref[h] = jnp.exp(s[0] - m1)
        b_ref[h] = jnp.exp(s[1] - m2) / z
        tau_ref[h] = jnp.broadcast_to(tau, tau_ref.shape[1:])


def _route(tn, w_pq, sub_keys):
    t = tn.shape[0]
    tr = ROUTE_TR
    qd = PEER_HEADS * PEER_QUERY_DIM
    out_spec = pl.BlockSpec((PEER_HEADS, PEER_N_KEYS, tr), lambda i: (0, 0, i))
    return pl.pallas_call(
        _route_kernel,
        grid=(t // tr,),
        in_specs=[
            pl.BlockSpec((tr, D_MODEL), lambda i: (i, 0)),
            pl.BlockSpec((D_MODEL, qd), lambda i: (0, 0)),
            pl.BlockSpec((2, PEER_N_KEYS, PEER_QUERY_DIM // 2), lambda i: (0, 0, 0)),
        ],
        out_specs=[out_spec] * 4 + [pl.BlockSpec((PEER_HEADS, 8, tr), lambda i: (0, 0, i))],
        out_shape=([jax.ShapeDtypeStruct((PEER_HEADS, PEER_N_KEYS, t), F32)] * 4
                   + [jax.ShapeDtypeStruct((PEER_HEADS, 8, t), F32)]),
        scratch_shapes=[pltpu.VMEM((PEER_TOPK, tr), F32)] * 2,
        compiler_params=_cparams(("parallel",)),
        name="route",
    )(tn, w_pq, sub_keys)


PEER_TP = 512
PEER_I1 = 8
PEER_EB = PEER_I1 * PEER_N_KEYS


def _peer_kernel(t_ref, u_ref, vt_ref, s1_ref, s2_ref, a_ref, b_ref, tau_ref, x1_ref, o_ref, acc):
    j = pl.program_id(1)

    @pl.when(j == 0)
    def _():
        acc[...] = jnp.zeros_like(acc)

    st = lax.dot_general(u_ref[...], t_ref[...], (((1,), (1,)), ((), ())),
                         preferred_element_type=F32)
    parts = []
    for i in range(PEER_I1):
        rows = slice(i * PEER_N_KEYS, (i + 1) * PEER_N_KEYS)
        pre = st[rows, :]
        act = 0.5 * pre * (1.0 + lax.erf(pre * (2.0 ** -0.5)))
        wgt = jnp.zeros_like(act)
        for h in range(PEER_HEADS):
            sel = (s1_ref[h, i:i + 1, :] + s2_ref[h]) >= tau_ref[h, 0:1, :]
            wgt = wgt + jnp.where(sel, b_ref[h], 0.0) * a_ref[h, i:i + 1, :]
        parts.append((act * wgt).astype(BF16))
    gated = jnp.concatenate(parts, axis=0)
    acc[...] += jnp.dot(vt_ref[...], gated, preferred_element_type=F32)

    @pl.when(j == pl.num_programs(1) - 1)
    def _():
        o_ref[...] = x1_ref[...] + acc[...].T


def _peer(tn, u_bf, vt_bf, s1, s2, a, b, tau, x1):
    t = tn.shape[0]
    tp, eb = PEER_TP, PEER_EB
    n_exp = u_bf.shape[0]
    key_spec = pl.BlockSpec((PEER_HEADS, PEER_N_KEYS, tp), lambda i, j: (0, 0, i))
    blk_spec = pl.BlockSpec((PEER_HEADS, PEER_I1, tp), lambda i, j: (0, j, i))
    return pl.pallas_call(
        _peer_kernel,
        grid=(t // tp, n_exp // eb),
        in_specs=[
            pl.BlockSpec((tp, D_MODEL), lambda i, j: (i, 0)),
            pl.BlockSpec((eb, D_MODEL), lambda i, j: (j, 0)),
            pl.BlockSpec((D_MODEL, eb), lambda i, j: (0, j)),
            blk_spec, key_spec, blk_spec, key_spec,
            pl.BlockSpec((PEER_HEADS, 8, tp), lambda i, j: (0, 0, i)),
            pl.BlockSpec((tp, D_MODEL), lambda i, j: (i, 0)),
        ],
        out_specs=pl.BlockSpec((tp, D_MODEL), lambda i, j: (i, 0)),
        out_shape=jax.ShapeDtypeStruct((t, D_MODEL), F32),
        scratch_shapes=[pltpu.VMEM((D_MODEL, tp), F32)],
        compiler_params=_cparams(("parallel", "arbitrary")),
        name="peer",
    )(tn, u_bf, vt_bf, s1, s2, a, b, tau, x1)


def _regroup_w_in(w_in):
    pad = jnp.zeros((D_MODEL, LANES - 2 * SSD_HEADS), w_in.dtype)
    return jnp.concatenate([w_in[:, OFF_XBC:OFF_DT], w_in[:, OFF_QKV:OFF_GATE], w_in[:, :OFF_XBC],
                            w_in[:, OFF_GATE:], w_in[:, OFF_DT:OFF_QKV], pad], axis=1).astype(BF16)


def _encoder_layer(x, g_mix, w_in, conv_w, conv_b, dt_bias, a_log, d_skip, ssd_norm_g,
                   q_norm_g, k_norm_g, rpb, w_out, g_ffn, w_pq, sub_keys, expert_u, expert_v):
    nseq, seqlen, _ = x.shape
    x2d = x.reshape(nseq * seqlen, D_MODEL)
    proj, dt_raw = _inproj(x2d, g_mix, _regroup_w_in(w_in))
    xbc = _conv(proj, conv_w, conv_b, nseq, seqlen)
    yf, yb = _ssd(xbc, dt_raw, dt_bias, a_log, nseq, seqlen)
    qn, kn = _qknorm(proj, q_norm_g, k_norm_g)
    yna = _natten(qn, kn, proj, rpb, nseq, seqlen)
    x1, tn = _outproj(yf, yb, xbc, proj, yna, x2d, w_out.astype(BF16), d_skip, ssd_norm_g, g_ffn)
    s1, s2, a, b, tau = _route(tn, w_pq.astype(BF16), sub_keys.astype(BF16))
    y = _peer(tn, expert_u.astype(BF16), expert_v.T.astype(BF16), s1, s2, a, b, tau, x1)
    return y.reshape(nseq, seqlen, D_MODEL)


def kernel(x_prompt, x_sample, g_mix, w_in, conv_w, conv_b, dt_bias, a_log, d_skip, ssd_norm_g, q_norm_g,
           k_norm_g, rpb, w_out, g_ffn, w_pq, sub_keys, expert_u, expert_v):
    assert x_prompt.shape[1:] == x_sample.shape[1:]
    x = jnp.concatenate([x_prompt, x_sample], axis=0)
    for i in range(g_mix.shape[0]):
        x = _encoder_layer(x, g_mix[i], w_in[i], conv_w[i], conv_b[i], dt_bias[i], a_log[i], d_skip[i],
                           ssd_norm_g[i], q_norm_g[i], k_norm_g[i], rpb[i], w_out[i], g_ffn[i], w_pq[i],
                           sub_keys[i], expert_u[i], expert_v[i])
    nb = x_prompt.shape[0]
    return (x[:nb], x[nb:])
```

```python
import functools

import jax
import jax.numpy as jnp
from jax import lax
from jax.experimental import pallas as pl
from jax.experimental.pallas import tpu as pltpu

F32 = jnp.float32
BF16 = jnp.bfloat16

D_MODEL = 1024
GRID_W = 64
EPS = 1e-6
D_INNER = 2048
SSD_HEAD_DIM = 64
SSD_HEADS = 32
SSD_GROUPS = 4
SSD_STATE = 128
CONV_W = 5
CHUNK = 128
CONV_DIM = D_INNER + 2 * SSD_GROUPS * SSD_STATE
NA_HEADS = 16
NA_HEAD_DIM = 64
NA_WIDTH = 1024
WIN_ROWS = 8
WIN_COLS = 16
PEER_HEADS = 8
PEER_N_KEYS = 128
PEER_TOPK = 16
PEER_QUERY_DIM = 256
OFF_XBC = D_INNER
OFF_DT = OFF_XBC + CONV_DIM
OFF_QKV = OFF_DT + 2 * SSD_HEADS
OFF_GATE = OFF_QKV + 3 * NA_WIDTH

LANES = 128
PROJ_COLS = CONV_DIM + 3 * NA_WIDTH + D_INNER + 2 * D_MODEL + LANES
COL_QKV = CONV_DIM
COL_Z = COL_QKV + 3 * NA_WIDTH
COL_GATE = COL_Z + D_INNER
COL_DT = COL_GATE + 2 * D_MODEL

VMEM_LIMIT = 56 * 1024 * 1024


def _cparams(sem):
    return pltpu.CompilerParams(dimension_semantics=sem, vmem_limit_bytes=VMEM_LIMIT)


def _rms(x, g):
    return x * lax.rsqrt(jnp.mean(x * x, axis=-1, keepdims=True) + EPS) * g


INPROJ_TM = 1024
INPROJ_TN = 1152


def _inproj_kernel(x_ref, g_ref, w_ref, o_ref, dt_ref, h_scr):
    j = pl.program_id(1)

    @pl.when(j == 0)
    def _():
        h_scr[...] = _rms(x_ref[...], g_ref[...]).astype(BF16)

    acc = jnp.dot(h_scr[...], w_ref[...], preferred_element_type=F32)
    o_ref[...] = acc.astype(BF16)

    @pl.when(j == pl.num_programs(1) - 1)
    def _():
        dt_ref[...] = acc[:, INPROJ_TN - LANES:]


def _inproj(x2d, g_mix, w_cat):
    t = x2d.shape[0]
    tm, tn = INPROJ_TM, INPROJ_TN
    assert t % tm == 0 and PROJ_COLS % tn == 0 and COL_DT == PROJ_COLS - LANES
    return pl.pallas_call(
        _inproj_kernel,
        grid=(t // tm, PROJ_COLS // tn),
        in_specs=[
            pl.BlockSpec((tm, D_MODEL), lambda i, j: (i, 0)),
            pl.BlockSpec((1, D_MODEL), lambda i, j: (0, 0)),
            pl.BlockSpec((D_MODEL, tn), lambda i, j: (0, j)),
        ],
        out_specs=[
            pl.BlockSpec((tm, tn), lambda i, j: (i, j)),
            pl.BlockSpec((tm, LANES), lambda i, j: (i, 0)),
        ],
        out_shape=[
            jax.ShapeDtypeStruct((t, PROJ_COLS), BF16),
            jax.ShapeDtypeStruct((t, LANES), F32),
        ],
        scratch_shapes=[pltpu.VMEM((tm, D_MODEL), BF16)],
        compiler_params=_cparams(("parallel", "arbitrary")),
        name="inproj",
    )(x2d, g_mix.reshape(1, D_MODEL), w_cat)


CONV_TL = 512
CONV_CB = 1024
HALO = 16
PAD = 8


def _conv_kernel(cur_ref, prev_ref, next_ref, w_ref, b_ref, o_ref, ext):
    i = pl.program_id(1)
    tl = cur_ref.shape[0]
    prev = prev_ref[...].astype(F32)[HALO - PAD:, :]
    nxt = next_ref[...].astype(F32)[:PAD, :]
    ext[0:PAD, :] = jnp.where(i > 0, prev, 0.0)
    ext[PAD:PAD + tl, :] = cur_ref[...].astype(F32)
    ext[PAD + tl:, :] = jnp.where(i < pl.num_programs(1) - 1, nxt, 0.0)
    acc = jnp.broadcast_to(b_ref[...], o_ref.shape)
    for k in range(CONV_W):
        acc = acc + w_ref[k:k + 1, :] * ext[pl.ds(PAD - CONV_W // 2 + k, tl), :]
    o_ref[...] = (acc * jax.nn.sigmoid(acc)).astype(BF16)


def _conv(proj, conv_w, conv_b, nseq, seqlen):
    t = proj.shape[0]
    tl, cb = CONV_TL, CONV_CB
    nl = seqlen // tl
    assert seqlen % tl == 0 and CONV_DIM % cb == 0
    return pl.pallas_call(
        _conv_kernel,
        grid=(nseq, nl, CONV_DIM // cb),
        in_specs=[
            pl.BlockSpec((tl, cb), lambda s, i, c: (s * nl + i, c)),
            pl.BlockSpec((HALO, cb), lambda s, i, c: (jnp.maximum((s * nl + i) * (tl // HALO) - 1, 0), c)),
            pl.BlockSpec((HALO, cb),
                         lambda s, i, c: (jnp.minimum((s * nl + i + 1) * (tl // HALO), t // HALO - 1), c)),
            pl.BlockSpec((CONV_W, cb), lambda s, i, c: (0, c)),
            pl.BlockSpec((1, cb), lambda s, i, c: (0, c)),
        ],
        out_specs=pl.BlockSpec((tl, cb), lambda s, i, c: (s * nl + i, c)),
        out_shape=jax.ShapeDtypeStruct((t, CONV_DIM), BF16),
        scratch_shapes=[pltpu.VMEM((tl + 2 * PAD, cb), F32)],
        compiler_params=_cparams(("parallel", "parallel", "parallel")),
        name="conv",
    )(proj, proj, proj, conv_w, conv_b.reshape(1, CONV_DIM))


def _cumsum_rows(x):
    row = lax.broadcasted_iota(jnp.int32, x.shape, 0)
    d = 1
    while d < x.shape[0]:
        x = x + jnp.where(row >= d, pltpu.roll(x, d, 0), 0.0)
        d *= 2
    return x


def _ssd_direction(reverse, xs_ref, b_ref, c_ref, dt_ref, dtb_ref, alog_ref, y_ref, st):
    row = lax.broadcasted_iota(jnp.int32, (CHUNK, CHUNK), 0)
    col = lax.broadcasted_iota(jnp.int32, (CHUNK, CHUNK), 1)
    lane1 = lax.broadcasted_iota(jnp.int32, (1, LANES), 1)
    dt_pre = dt_ref[...] + dtb_ref[...]
    dt = jnp.maximum(dt_pre, 0.0) + jnp.log1p(jnp.exp(-jnp.abs(dt_pre)))
    dta = dt * (-jnp.exp(alog_ref[...]))
    cs = _cumsum_rows(dta)
    total = cs[CHUNK - 1:CHUNK, :]
    if reverse:
        cum = total - cs + dta
        wexp = cs - dta
        tri = col >= row
    else:
        cum = cs
        wexp = total - cs
        tri = row >= col
    w = jnp.exp(wexp) * dt
    cum_t = cum.T
    dt_t = dt.T
    w_t = w.T
    dec = jnp.exp(total)
    base = SSD_HEADS if reverse else 0
    pairs_per_group = SSD_HEADS // SSD_GROUPS // 2
    for g in range(SSD_GROUPS):
        bg = b_ref[:, g * SSD_STATE:(g + 1) * SSD_STATE]
        cg = c_ref[:, g * SSD_STATE:(g + 1) * SSD_STATE]
        gmat = lax.dot_general(cg, bg, (((1,), (1,)), ((), ())), preferred_element_type=F32)
        bg_t = bg.astype(F32).T
        cg32 = cg.astype(F32)
        for hp in range(pairs_per_group):
            j = g * pairs_per_group + hp
            sl = slice(j * LANES, (j + 1) * LANES)
            xs_pair = xs_ref[:, sl].astype(F32)
            st_pair = st[:, sl]
            y_acc = jnp.zeros((CHUNK, LANES), F32)
            s_acc = jnp.zeros((SSD_STATE, LANES), F32)
            dec_pair = jnp.zeros((1, LANES), F32)
            for e in range(2):
                ln = base + 2 * j + e
                in_head = (col // SSD_HEAD_DIM) == e
                colb = jnp.broadcast_to(cum[:, ln:ln + 1], (CHUNK, CHUNK))
                decay = jnp.where(tri, jnp.exp(colb - cum_t[ln:ln + 1, :]), 0.0)
                m = gmat * decay * dt_t[ln:ln + 1, :]
                ce = cg32 * jnp.exp(colb)
                lhs = jnp.concatenate([m, ce], axis=1).astype(BF16)
                xs_m = jnp.where(in_head, xs_pair, 0.0).astype(BF16)
                st_m = jnp.where(in_head, st_pair, 0.0).astype(BF16)
                rhs = jnp.concatenate([xs_m, st_m], axis=0)
                y_acc = y_acc + jnp.dot(lhs, rhs, preferred_element_type=F32)
                btw = (bg_t * w_t[ln:ln + 1, :]).astype(BF16)
                s_acc = s_acc + jnp.dot(btw, xs_m, preferred_element_type=F32)
                dec_pair = jnp.where((lane1 // SSD_HEAD_DIM) == e,
                                     jnp.broadcast_to(dec[:, ln:ln + 1], (1, LANES)), dec_pair)
            y_ref[:, sl] = y_acc.astype(BF16)
            st[:, sl] = st_pair * dec_pair + s_acc


def _ssd_kernel(xs_f, b_f, c_f, dt_f, xs_b, b_b, c_b, dt_b, dtb_ref, alog_ref, yf_ref, yb_ref, st_f, st_b):
    @pl.when(pl.program_id(1) == 0)
    def _():
        st_f[...] = jnp.zeros_like(st_f)
        st_b[...] = jnp.zeros_like(st_b)

    _ssd_direction(False, xs_f, b_f, c_f, dt_f, dtb_ref, alog_ref, yf_ref, st_f)
    _ssd_direction(True, xs_b, b_b, c_b, dt_b, dtb_ref, alog_ref, yb_ref, st_b)


def _ssd(xbc, dt_raw, dt_bias, a_log, nseq, seqlen):
    t = xbc.shape[0]
    nc = seqlen // CHUNK
    bn = SSD_GROUPS * SSD_STATE
    fwd = lambda s, c: s * nc + c
    bwd = lambda s, c: s * nc + (nc - 1 - c)

    def specs(rb):
        return [
            pl.BlockSpec((CHUNK, D_INNER), lambda s, c: (rb(s, c), 0)),
            pl.BlockSpec((CHUNK, bn), lambda s, c: (rb(s, c), D_INNER // bn)),
            pl.BlockSpec((CHUNK, bn), lambda s, c: (rb(s, c), D_INNER // bn + 1)),
        ]

    dt_spec = lambda rb: pl.BlockSpec((CHUNK, LANES), lambda s, c: (rb(s, c), 0))
    row_spec = pl.BlockSpec((1, LANES), lambda s, c: (0, 0))
    pad = jnp.zeros((LANES - 2 * SSD_HEADS,), F32)
    dtb = jnp.concatenate([dt_bias.reshape(-1), pad]).reshape(1, LANES)
    alog = jnp.concatenate([a_log.reshape(-1), pad]).reshape(1, LANES)
    return pl.pallas_call(
        _ssd_kernel,
        grid=(nseq, nc),
        in_specs=specs(fwd) + [dt_spec(fwd)] + specs(bwd) + [dt_spec(bwd)] + [row_spec, row_spec],
        out_specs=[
            pl.BlockSpec((CHUNK, D_INNER), lambda s, c: (fwd(s, c), 0)),
            pl.BlockSpec((CHUNK, D_INNER), lambda s, c: (bwd(s, c), 0)),
        ],
        out_shape=[jax.ShapeDtypeStruct((t, D_INNER), BF16)] * 2,
        scratch_shapes=[pltpu.VMEM((SSD_STATE, D_INNER), F32)] * 2,
        compiler_params=_cparams(("parallel", "arbitrary")),
        name="ssd",
    )(xbc, xbc, xbc, dt_raw, xbc, xbc, xbc, dt_raw, dtb, alog)


QKN_TM = 512


def _qknorm_kernel(q_ref, k_ref, gq_ref, gk_ref, qo_ref, ko_ref):
    r = lax.broadcasted_iota(jnp.int32, (LANES, LANES), 0) // NA_HEAD_DIM
    c = lax.broadcasted_iota(jnp.int32, (LANES, LANES), 1) // NA_HEAD_DIM
    same_head = jnp.where(r == c, 1.0, 0.0).astype(BF16)
    for src, g_ref, dst, scale in ((q_ref, gq_ref, qo_ref, NA_HEAD_DIM ** -0.5), (k_ref, gk_ref, ko_ref, None)):
        for p in range(NA_WIDTH // LANES):
            sl = slice(p * LANES, (p + 1) * LANES)
            x = src[:, sl].astype(F32)
            sq = x * x
            hi = sq.astype(BF16)
            lo = (sq - hi.astype(F32)).astype(BF16)
            ss = (jnp.dot(hi, same_head, preferred_element_type=F32)
                  + jnp.dot(lo, same_head, preferred_element_type=F32))
            y = x * lax.rsqrt(ss / NA_HEAD_DIM + EPS) * g_ref[...]
            if scale is not None:
                y = y * scale
            dst[:, sl] = y.astype(BF16)


def _qknorm(proj, q_norm_g, k_norm_g):
    t = proj.shape[0]
    tm = QKN_TM
    reps = LANES // NA_HEAD_DIM
    gq = jnp.tile(q_norm_g, reps).reshape(1, LANES)
    gk = jnp.tile(k_norm_g, reps).reshape(1, LANES)
    qcol = COL_QKV // NA_WIDTH
    return pl.pallas_call(
        _qknorm_kernel,
        grid=(t // tm,),
        in_specs=[
            pl.BlockSpec((tm, NA_WIDTH), lambda i: (i, qcol)),
            pl.BlockSpec((tm, NA_WIDTH), lambda i: (i, qcol + 1)),
            pl.BlockSpec((1, LANES), lambda i: (0, 0)),
            pl.BlockSpec((1, LANES), lambda i: (0, 0)),
        ],
        out_specs=[pl.BlockSpec((tm, NA_WIDTH), lambda i: (i, 0))] * 2,
        out_shape=[jax.ShapeDtypeStruct((t, NA_WIDTH), BF16)] * 2,
        compiler_params=_cparams(("parallel",)),
        name="qknorm",
    )(proj, proj, gq, gk)


def _natten_kernel(*refs):
    q_ref = refs[0]
    k_refs = refs[1:1 + WIN_ROWS]
    v_refs = refs[1 + WIN_ROWS:1 + 2 * WIN_ROWS]
    bias_ref = refs[1 + 2 * WIN_ROWS]
    o_ref = refs[2 + 2 * WIN_ROWS]
    first = lax.broadcasted_iota(jnp.int32, (GRID_W, LANES), 1) < NA_HEAD_DIM
    for p in range(NA_WIDTH // LANES):
        sl = slice(p * LANES, (p + 1) * LANES)
        qp = q_ref[:, sl].astype(F32)
        qrows = jnp.concatenate([jnp.where(first, qp, 0.0), jnp.where(first, 0.0, qp)], axis=0).astype(BF16)
        kp = jnp.concatenate([r[:, sl] for r in k_refs], axis=0)
        vp = jnp.concatenate([r[:, sl] for r in v_refs], axis=0)
        st = lax.dot_general(kp, qrows, (((1,), (1,)), ((), ())), preferred_element_type=F32)
        st = st + bias_ref[0, p]
        m = jnp.max(st, axis=0, keepdims=True)
        pe = jnp.exp(st - m)
        inv = 1.0 / jnp.sum(pe, axis=0, keepdims=True)
        pn = (pe * inv).astype(BF16)
        o2 = lax.dot_general(pn, vp, (((0,), (0,)), ((), ())), preferred_element_type=F32)
        o_ref[:, sl] = jnp.where(first, o2[:GRID_W], o2[GRID_W:]).astype(BF16)


def _na_bias_table(rpb):
    cols = jnp.arange(GRID_W)
    c0 = jnp.clip(cols - WIN_COLS // 2, 0, GRID_W - WIN_COLS)
    j = jnp.arange(GRID_W)
    valid = (j[None, :] >= c0[:, None]) & (j[None, :] < c0[:, None] + WIN_COLS)
    k = jnp.arange(2 * WIN_COLS - 1)
    onehot = (k[:, None, None] == j[None, None, :] - cols[None, :, None] + (WIN_COLS - 1)).astype(F32)
    toep = jnp.einsum("hrk,kcj->hrcj", rpb.astype(F32), onehot, precision=lax.Precision.HIGHEST)
    toep = jnp.where(valid[None, None], toep, -jnp.inf)
    per_delta = jnp.stack([toep[:, WIN_ROWS - 1 - d:2 * WIN_ROWS - 1 - d] for d in range(WIN_ROWS)], axis=0)
    hp = LANES // NA_HEAD_DIM
    t6 = per_delta.reshape(WIN_ROWS, NA_HEADS // hp, hp, WIN_ROWS, GRID_W, GRID_W)
    t6 = jnp.transpose(t6, (0, 1, 3, 5, 2, 4))
    return t6.reshape(WIN_ROWS, NA_HEADS // hp, WIN_ROWS * GRID_W, hp * GRID_W)


def _natten(qn, kn, proj, rpb, nseq, seqlen):
    t = qn.shape[0]
    rows = seqlen // GRID_W
    assert rows >= WIN_ROWS
    r0 = lambda r: jnp.clip(r - WIN_ROWS // 2, 0, rows - WIN_ROWS)
    vcol = COL_QKV // NA_WIDTH + 2
    tbl = _na_bias_table(rpb)

    def win_spec(i, colblk):
        return pl.BlockSpec((GRID_W, NA_WIDTH), lambda s, r: (s * rows + r0(r) + i, colblk))

    return pl.pallas_call(
        _natten_kernel,
        grid=(nseq, rows),
        in_specs=([pl.BlockSpec((GRID_W, NA_WIDTH), lambda s, r: (s * rows + r, 0))]
                  + [win_spec(i, 0) for i in range(WIN_ROWS)]
                  + [win_spec(i, vcol) for i in range(WIN_ROWS)]
                  + [pl.BlockSpec((1,) + tbl.shape[1:], lambda s, r: (r - r0(r), 0, 0, 0))]),
        out_specs=pl.BlockSpec((GRID_W, NA_WIDTH), lambda s, r: (s * rows + r, 0)),
        out_shape=jax.ShapeDtypeStruct((t, NA_WIDTH), BF16),
        compiler_params=_cparams(("parallel", "arbitrary")),
        name="natten",
    )(qn, *([kn] * WIN_ROWS), *([proj] * WIN_ROWS), tbl)


OUTPROJ_TM = 512


def _outproj_kernel(yf_ref, yb_ref, xs_ref, z_ref, g1_ref, g2_ref, yna_ref, x_ref, w1_ref, w2_ref,
                    dsk_ref, ng_ref, gffn_ref, x1_ref, t_ref):
    y = yf_ref[...].astype(F32) + yb_ref[...].astype(F32) + dsk_ref[...] * xs_ref[...].astype(F32)
    y = y * jax.nn.silu(z_ref[...].astype(F32))
    yn = _rms(y, ng_ref[...]).astype(BF16)
    br_ssd = jnp.dot(yn, w1_ref[...], preferred_element_type=F32)
    br_na = jnp.dot(yna_ref[...], w2_ref[...], preferred_element_type=F32)
    mix = (jax.nn.sigmoid(g1_ref[...].astype(F32)) * br_ssd
           + jax.nn.sigmoid(g2_ref[...].astype(F32)) * br_na)
    x1 = x_ref[...] + mix
    x1_ref[...] = x1
    t_ref[...] = _rms(x1, gffn_ref[...]).astype(BF16)


def _outproj(yf, yb, xbc, proj, yna, x2d, w_out, d_skip, ssd_norm_g, g_ffn):
    t = x2d.shape[0]
    tm = OUTPROJ_TM
    tok = lambda w, cb: pl.BlockSpec((tm, w), lambda i: (i, cb))
    row = lambda w: pl.BlockSpec((1, w), lambda i: (0, 0))
    dsk = jnp.repeat(d_skip, SSD_HEAD_DIM).reshape(1, D_INNER)
    return pl.pallas_call(
        _outproj_kernel,
        grid=(t // tm,),
        in_specs=[
            tok(D_INNER, 0), tok(D_INNER, 0), tok(D_INNER, 0),
            tok(D_INNER, COL_Z // D_INNER),
            tok(D_MODEL, COL_GATE // D_MODEL), tok(D_MODEL, COL_GATE // D_MODEL + 1),
            tok(NA_WIDTH, 0), tok(D_MODEL, 0),
            pl.BlockSpec((D_INNER, D_MODEL), lambda i: (0, 0)),
            pl.BlockSpec((NA_WIDTH, D_MODEL), lambda i: (D_INNER // NA_WIDTH, 0)),
            row(D_INNER), row(D_INNER), row(D_MODEL),
        ],
        out_specs=[tok(D_MODEL, 0), tok(D_MODEL, 0)],
        out_shape=[jax.ShapeDtypeStruct((t, D_MODEL), F32), jax.ShapeDtypeStruct((t, D_MODEL), BF16)],
        compiler_params=_cparams(("parallel",)),
        name="outproj",
    )(yf, yb, xbc, proj, proj, proj, yna, x2d, w_out, w_out, dsk,
      ssd_norm_g.reshape(1, D_INNER), g_ffn.reshape(1, D_MODEL))


ROUTE_TR = 256
NEG_INF = float("-inf")
RANK_NONE = 64.0


def _top_values(s, k, out_ref=None, want_rank=False):
    mx = None
    rank = jnp.full(s.shape, RANK_NONE, F32) if want_rank else None
    for r in range(k):
        mx = jnp.max(s, axis=0, keepdims=True)
        if out_ref is not None:
            out_ref[r:r + 1, :] = mx
        hit = s == mx
        if want_rank:
            rank = jnp.where(hit, float(r), rank)
        s = jnp.where(hit, NEG_INF, s)
    return mx, rank


def _route_kernel(t_ref, wpq_ref, sk_ref, rank2_ref, b_ref, cnt_ref, a_ref, top1, top2):
    half = PEER_QUERY_DIM // 2
    sub = 8
    row = lax.broadcasted_iota(jnp.int32, (sub, t_ref.shape[0]), 0)
    q = jnp.dot(t_ref[...], wpq_ref[...], preferred_element_type=F32).astype(BF16)
    for h in range(PEER_HEADS):
        s = []
        for side in range(2):
            qb = q[:, (2 * h + side) * half:(2 * h + side + 1) * half]
            s.append(lax.dot_general(sk_ref[side], qb, (((1,), (1,)), ((), ())),
                                     preferred_element_type=F32))
        _top_values(s[0], PEER_TOPK, top1)
        _, rank2 = _top_values(s[1], PEER_TOPK, top2, want_rank=True)
        tiles = [top1[0:1, :] + top2[0:sub, :], top1[0:1, :] + top2[sub:, :]]
        for r1 in range(1, sub):
            tile = top1[r1:r1 + 1, :] + top2[0:sub, :]
            tiles.append(jnp.where(row < PEER_TOPK // (r1 + 1), tile, NEG_INF))
        tiles.append(top1[sub:, :] + top2[0:1, :])
        cand = jnp.concatenate(tiles, axis=0)
        tau, _ = _top_values(cand, PEER_TOPK)
        m1 = top1[0:1, :]
        m2 = top2[0:1, :]
        z = jnp.sum(jnp.where(cand >= tau, jnp.exp(cand - (m1 + m2)), 0.0), axis=0, keepdims=True)
        cnt = jnp.zeros_like(s[0])
        for r2 in range(PEER_TOPK):
            cnt = cnt + jnp.where((s[0] + top2[r2:r2 + 1, :]) >= tau, 1.0, 0.0)
        rank2_ref[h] = rank2.astype(BF16)
        b_ref[h] = (jnp.exp(s[1] - m2) / z).astype(BF16)
        cnt_ref[h] = cnt
        a_ref[h] = jnp.exp(s[0] - m1)


def _route(tn, w_pq, sub_keys):
    t = tn.shape[0]
    tr = ROUTE_TR
    qd = PEER_HEADS * PEER_QUERY_DIM
    out_spec = pl.BlockSpec((PEER_HEADS, PEER_N_KEYS, tr), lambda i: (0, 0, i))
    key_shape = (PEER_HEADS, PEER_N_KEYS, t)
    return pl.pallas_call(
        _route_kernel,
        grid=(t // tr,),
        in_specs=[
            pl.BlockSpec((tr, D_MODEL), lambda i: (i, 0)),
            pl.BlockSpec((D_MODEL, qd), lambda i: (0, 0)),
            pl.BlockSpec((2, PEER_N_KEYS, PEER_QUERY_DIM // 2), lambda i: (0, 0, 0)),
        ],
        out_specs=[out_spec] * 4,
        out_shape=[jax.ShapeDtypeStruct(key_shape, BF16), jax.ShapeDtypeStruct(key_shape, BF16),
                   jax.ShapeDtypeStruct(key_shape, F32), jax.ShapeDtypeStruct(key_shape, F32)],
        scratch_shapes=[pltpu.VMEM((PEER_TOPK, tr), F32)] * 2,
        compiler_params=_cparams(("parallel",)),
        name="route",
    )(tn, w_pq, sub_keys)


PEER_TP = 512
PEER_I1 = 8
PEER_EB = PEER_I1 * PEER_N_KEYS
BF16_ROWS = 16


def _bf16_rows(row, nrows):
    one = jnp.broadcast_to(row, (BF16_ROWS, LANES)).astype(BF16)
    return pltpu.repeat(one, nrows // BF16_ROWS, axis=0)


def _peer_kernel(t_ref, u_ref, vt_ref, rank2_ref, b_ref, cnt_ref, a_ref, x1_ref, o_ref, acc, st, gated):
    j = pl.program_id(1)

    @pl.when(j == 0)
    def _():
        acc[...] = jnp.zeros_like(acc)

    st[...] = lax.dot_general(u_ref[...], t_ref[...], (((1,), (1,)), ((), ())),
                              preferred_element_type=F32)
    for i in range(PEER_I1):
        rows = slice(i * PEER_N_KEYS, (i + 1) * PEER_N_KEYS)
        for ck in range(t_ref.shape[0] // LANES):
            cols = slice(ck * LANES, (ck + 1) * LANES)
            pre = st[rows, cols]
            act = (0.5 * pre * (1.0 + lax.erf(pre * (2.0 ** -0.5)))).astype(BF16)
            wgt = jnp.zeros((PEER_N_KEYS, LANES), BF16)
            for h in range(PEER_HEADS):
                cnt = _bf16_rows(cnt_ref[h, i:i + 1, cols], PEER_N_KEYS)
                a1 = _bf16_rows(a_ref[h, i:i + 1, cols], PEER_N_KEYS)
                b2 = b_ref[h, :, cols]
                wgt = wgt + jnp.where(rank2_ref[h, :, cols] < cnt, b2, jnp.zeros_like(b2)) * a1
            gated[rows, cols] = act * wgt
    acc[...] += jnp.dot(vt_ref[...], gated[...], preferred_element_type=F32)

    @pl.when(j == pl.num_programs(1) - 1)
    def _():
        o_ref[...] = x1_ref[...] + acc[...].T


def _peer(tn, u_bf, vt_bf, rank2, b, cnt, a, x1):
    t = tn.shape[0]
    tp, eb = PEER_TP, PEER_EB
    n_exp = u_bf.shape[0]
    key_spec = pl.BlockSpec((PEER_HEADS, PEER_N_KEYS, tp), lambda i, j: (0, 0, i))
    blk_spec = pl.BlockSpec((PEER_HEADS, PEER_I1, tp), lambda i, j: (0, j, i))
    return pl.pallas_call(
        _peer_kernel,
        grid=(t // tp, n_exp // eb),
        in_specs=[
            pl.BlockSpec((tp, D_MODEL), lambda i, j: (i, 0)),
            pl.BlockSpec((eb, D_MODEL), lambda i, j: (j, 0)),
            pl.BlockSpec((D_MODEL, eb), lambda i, j: (0, j)),
            key_spec, key_spec, blk_spec, blk_spec,
            pl.BlockSpec((tp, D_MODEL), lambda i, j: (i, 0)),
        ],
        out_specs=pl.BlockSpec((tp, D_MODEL), lambda i, j: (i, 0)),
        out_shape=jax.ShapeDtypeStruct((t, D_MODEL), F32),
        scratch_shapes=[pltpu.VMEM((D_MODEL, tp), F32), pltpu.VMEM((eb, tp), F32), pltpu.VMEM((eb, tp), BF16)],
        compiler_params=_cparams(("parallel", "arbitrary")),
        name="peer",
    )(tn, u_bf, vt_bf, rank2, b, cnt, a, x1)


def _regroup_w_in(w_in):
    pad = jnp.zeros((D_MODEL, LANES - 2 * SSD_HEADS), w_in.dtype)
    return jnp.concatenate([w_in[:, OFF_XBC:OFF_DT], w_in[:, OFF_QKV:OFF_GATE], w_in[:, :OFF_XBC],
                            w_in[:, OFF_GATE:], w_in[:, OFF_DT:OFF_QKV], pad], axis=1).astype(BF16)


def _encoder_layer(x, g_mix, w_in, conv_w, conv_b, dt_bias, a_log, d_skip, ssd_norm_g,
                   q_norm_g, k_norm_g, rpb, w_out, g_ffn, w_pq, sub_keys, expert_u, expert_v):
    nseq, seqlen, _ = x.shape
    x2d = x.reshape(nseq * seqlen, D_MODEL)
    proj, dt_raw = _inproj(x2d, g_mix, _regroup_w_in(w_in))
    xbc = _conv(proj, conv_w, conv_b, nseq, seqlen)
    yf, yb = _ssd(xbc, dt_raw, dt_bias, a_log, nseq, seqlen)
    qn, kn = _qknorm(proj, q_norm_g, k_norm_g)
    yna = _natten(qn, kn, proj, rpb, nseq, seqlen)
    x1, tn = _outproj(yf, yb, xbc, proj, yna, x2d, w_out.astype(BF16), d_skip, ssd_norm_g, g_ffn)
    rank2, b, cnt, a = _route(tn, w_pq.astype(BF16), sub_keys.astype(BF16))
    y = _peer(tn, expert_u.astype(BF16), expert_v.T.astype(BF16), rank2, b, cnt, a, x1)
    return y.reshape(nseq, seqlen, D_MODEL)


def kernel(x_prompt, x_sample, g_mix, w_in, conv_w, conv_b, dt_bias, a_log, d_skip, ssd_norm_g, q_norm_g,
           k_norm_g, rpb, w_out, g_ffn, w_pq, sub_keys, expert_u, expert_v):
    assert x_prompt.shape[1:] == x_sample.shape[1:]
    x = jnp.concatenate([x_prompt, x_sample], axis=0)
    for i in range(g_mix.shape[0]):
        x = _encoder_layer(x, g_mix[i], w_in[i], conv_w[i], conv_b[i], dt_bias[i], a_log[i], d_skip[i],
                           ssd_norm_g[i], q_norm_g[i], k_norm_g[i], rpb[i], w_out[i], g_ffn[i], w_pq[i],
                           sub_keys[i], expert_u[i], expert_v[i])
    nb = x_prompt.shape[0]
    return (x[:nb], x[nb:])
```

```python
import functools

import jax
import jax.numpy as jnp
from jax import lax
from jax.experimental import pallas as pl
from jax.experimental.pallas import tpu as pltpu

F32 = jnp.float32
BF16 = jnp.bfloat16

D_MODEL = 1024
GRID_W = 64
EPS = 1e-6
D_INNER = 2048
SSD_HEAD_DIM = 64
SSD_HEADS = 32
SSD_GROUPS = 4
SSD_STATE = 128
CONV_W = 5
CHUNK = 128
CONV_DIM = D_INNER + 2 * SSD_GROUPS * SSD_STATE
NA_HEADS = 16
NA_HEAD_DIM = 64
NA_WIDTH = 1024
WIN_ROWS = 8
WIN_COLS = 16
PEER_HEADS = 8
PEER_N_KEYS = 128
PEER_TOPK = 16
PEER_QUERY_DIM = 256
OFF_XBC = D_INNER
OFF_DT = OFF_XBC + CONV_DIM
OFF_QKV = OFF_DT + 2 * SSD_HEADS
OFF_GATE = OFF_QKV + 3 * NA_WIDTH

LANES = 128
PROJ_COLS = CONV_DIM + 3 * NA_WIDTH + D_INNER + 2 * D_MODEL + LANES
COL_QKV = CONV_DIM
COL_Z = COL_QKV + 3 * NA_WIDTH
COL_GATE = COL_Z + D_INNER
COL_DT = COL_GATE + 2 * D_MODEL

VMEM_LIMIT = 56 * 1024 * 1024


def _cparams(sem, flags=None):
    return pltpu.CompilerParams(dimension_semantics=sem, vmem_limit_bytes=VMEM_LIMIT, flags=flags)


def _rms(x, g):
    return x * lax.rsqrt(jnp.mean(x * x, axis=-1, keepdims=True) + EPS) * g


INPROJ_TM = 1024
INPROJ_TN = 1152


def _inproj_kernel(x_ref, g_ref, w_ref, o_ref, dt_ref, h_scr):
    j = pl.program_id(1)

    @pl.when(j == 0)
    def _():
        h_scr[...] = _rms(x_ref[...], g_ref[...]).astype(BF16)

    acc = jnp.dot(h_scr[...], w_ref[...], preferred_element_type=F32)
    o_ref[...] = acc.astype(BF16)

    @pl.when(j == pl.num_programs(1) - 1)
    def _():
        dt_ref[...] = acc[:, INPROJ_TN - LANES:]


def _inproj(x2d, g_mix, w_cat):
    t = x2d.shape[0]
    tm, tn = INPROJ_TM, INPROJ_TN
    assert t % tm == 0 and PROJ_COLS % tn == 0 and COL_DT == PROJ_COLS - LANES
    return pl.pallas_call(
        _inproj_kernel,
        grid=(t // tm, PROJ_COLS // tn),
        in_specs=[
            pl.BlockSpec((tm, D_MODEL), lambda i, j: (i, 0)),
            pl.BlockSpec((1, D_MODEL), lambda i, j: (0, 0)),
            pl.BlockSpec((D_MODEL, tn), lambda i, j: (0, j)),
        ],
        out_specs=[
            pl.BlockSpec((tm, tn), lambda i, j: (i, j)),
            pl.BlockSpec((tm, LANES), lambda i, j: (i, 0)),
        ],
        out_shape=[
            jax.ShapeDtypeStruct((t, PROJ_COLS), BF16),
            jax.ShapeDtypeStruct((t, LANES), F32),
        ],
        scratch_shapes=[pltpu.VMEM((tm, D_MODEL), BF16)],
        compiler_params=_cparams(("parallel", "arbitrary")),
        name="inproj",
    )(x2d, g_mix.reshape(1, D_MODEL), w_cat)


CONV_TL = 512
CONV_CB = 1024
HALO = 16
PAD = 8


def _conv_kernel(cur_ref, prev_ref, next_ref, w_ref, b_ref, o_ref, ext):
    i = pl.program_id(1)
    tl = cur_ref.shape[0]
    prev = prev_ref[...].astype(F32)[HALO - PAD:, :]
    nxt = next_ref[...].astype(F32)[:PAD, :]
    ext[0:PAD, :] = jnp.where(i > 0, prev, 0.0)
    ext[PAD:PAD + tl, :] = cur_ref[...].astype(F32)
    ext[PAD + tl:, :] = jnp.where(i < pl.num_programs(1) - 1, nxt, 0.0)
    acc = jnp.broadcast_to(b_ref[...], o_ref.shape)
    for k in range(CONV_W):
        acc = acc + w_ref[k:k + 1, :] * ext[pl.ds(PAD - CONV_W // 2 + k, tl), :]
    o_ref[...] = (acc * jax.nn.sigmoid(acc)).astype(BF16)


def _conv(proj, conv_w, conv_b, nseq, seqlen):
    t = proj.shape[0]
    tl, cb = CONV_TL, CONV_CB
    nl = seqlen // tl
    assert seqlen % tl == 0 and CONV_DIM % cb == 0
    return pl.pallas_call(
        _conv_kernel,
        grid=(nseq, nl, CONV_DIM // cb),
        in_specs=[
            pl.BlockSpec((tl, cb), lambda s, i, c: (s * nl + i, c)),
            pl.BlockSpec((HALO, cb), lambda s, i, c: (jnp.maximum((s * nl + i) * (tl // HALO) - 1, 0), c)),
            pl.BlockSpec((HALO, cb),
                         lambda s, i, c: (jnp.minimum((s * nl + i + 1) * (tl // HALO), t // HALO - 1), c)),
            pl.BlockSpec((CONV_W, cb), lambda s, i, c: (0, c)),
            pl.BlockSpec((1, cb), lambda s, i, c: (0, c)),
        ],
        out_specs=pl.BlockSpec((tl, cb), lambda s, i, c: (s * nl + i, c)),
        out_shape=jax.ShapeDtypeStruct((t, CONV_DIM), BF16),
        scratch_shapes=[pltpu.VMEM((tl + 2 * PAD, cb), F32)],
        compiler_params=_cparams(("parallel", "parallel", "parallel")),
        name="conv",
    )(proj, proj, proj, conv_w, conv_b.reshape(1, CONV_DIM))


def _cumsum_rows(x):
    row = lax.broadcasted_iota(jnp.int32, x.shape, 0)
    d = 1
    while d < x.shape[0]:
        x = x + jnp.where(row >= d, pltpu.roll(x, d, 0), 0.0)
        d *= 2
    return x


def _ssd_direction(reverse, xs_ref, b_ref, c_ref, dt_ref, dtb_ref, alog_ref, y_ref, st):
    row = lax.broadcasted_iota(jnp.int32, (CHUNK, CHUNK), 0)
    col = lax.broadcasted_iota(jnp.int32, (CHUNK, CHUNK), 1)
    lane1 = lax.broadcasted_iota(jnp.int32, (1, LANES), 1)
    dt_pre = dt_ref[...] + dtb_ref[...]
    dt = jnp.maximum(dt_pre, 0.0) + jnp.log1p(jnp.exp(-jnp.abs(dt_pre)))
    dta = dt * (-jnp.exp(alog_ref[...]))
    cs = _cumsum_rows(dta)
    total = cs[CHUNK - 1:CHUNK, :]
    if reverse:
        cum = total - cs + dta
        wexp = cs - dta
        tri = col >= row
    else:
        cum = cs
        wexp = total - cs
        tri = row >= col
    w = jnp.exp(wexp) * dt
    cum_t = cum.T
    dt_t = dt.T
    w_t = w.T
    dec = jnp.exp(total)
    base = SSD_HEADS if reverse else 0
    pairs_per_group = SSD_HEADS // SSD_GROUPS // 2
    for g in range(SSD_GROUPS):
        bg = b_ref[:, g * SSD_STATE:(g + 1) * SSD_STATE]
        cg = c_ref[:, g * SSD_STATE:(g + 1) * SSD_STATE]
        gmat = lax.dot_general(cg, bg, (((1,), (1,)), ((), ())), preferred_element_type=F32)
        bg_t = bg.astype(F32).T
        cg32 = cg.astype(F32)
        for hp in range(pairs_per_group):
            j = g * pairs_per_group + hp
            sl = slice(j * LANES, (j + 1) * LANES)
            xs_pair = xs_ref[:, sl].astype(F32)
            st_pair = st[:, sl]
            y_acc = jnp.zeros((CHUNK, LANES), F32)
            s_acc = jnp.zeros((SSD_STATE, LANES), F32)
            dec_pair = jnp.zeros((1, LANES), F32)
            for e in range(2):
                ln = base + 2 * j + e
                in_head = (col // SSD_HEAD_DIM) == e
                colb = jnp.broadcast_to(cum[:, ln:ln + 1], (CHUNK, CHUNK))
                decay = jnp.where(tri, jnp.exp(colb - cum_t[ln:ln + 1, :]), 0.0)
                m = gmat * decay * dt_t[ln:ln + 1, :]
                ce = cg32 * jnp.exp(colb)
                lhs = jnp.concatenate([m, ce], axis=1).astype(BF16)
                xs_m = jnp.where(in_head, xs_pair, 0.0).astype(BF16)
                st_m = jnp.where(in_head, st_pair, 0.0).astype(BF16)
                rhs = jnp.concatenate([xs_m, st_m], axis=0)
                y_acc = y_acc + jnp.dot(lhs, rhs, preferred_element_type=F32)
                btw = (bg_t * w_t[ln:ln + 1, :]).astype(BF16)
                s_acc = s_acc + jnp.dot(btw, xs_m, preferred_element_type=F32)
                dec_pair = jnp.where((lane1 // SSD_HEAD_DIM) == e,
                                     jnp.broadcast_to(dec[:, ln:ln + 1], (1, LANES)), dec_pair)
            y_ref[:, sl] = y_acc.astype(BF16)
            st[:, sl] = st_pair * dec_pair + s_acc


def _ssd_kernel(xs_f, b_f, c_f, dt_f, xs_b, b_b, c_b, dt_b, dtb_ref, alog_ref, yf_ref, yb_ref, st_f, st_b):
    @pl.when(pl.program_id(1) == 0)
    def _():
        st_f[...] = jnp.zeros_like(st_f)
        st_b[...] = jnp.zeros_like(st_b)

    _ssd_direction(False, xs_f, b_f, c_f, dt_f, dtb_ref, alog_ref, yf_ref, st_f)
    _ssd_direction(True, xs_b, b_b, c_b, dt_b, dtb_ref, alog_ref, yb_ref, st_b)


def _ssd(xbc, dt_raw, dt_bias, a_log, nseq, seqlen):
    t = xbc.shape[0]
    nc = seqlen // CHUNK
    bn = SSD_GROUPS * SSD_STATE
    fwd = lambda s, c: s * nc + c
    bwd = lambda s, c: s * nc + (nc - 1 - c)

    def specs(rb):
        return [
            pl.BlockSpec((CHUNK, D_INNER), lambda s, c: (rb(s, c), 0)),
            pl.BlockSpec((CHUNK, bn), lambda s, c: (rb(s, c), D_INNER // bn)),
            pl.BlockSpec((CHUNK, bn), lambda s, c: (rb(s, c), D_INNER // bn + 1)),
        ]

    dt_spec = lambda rb: pl.BlockSpec((CHUNK, LANES), lambda s, c: (rb(s, c), 0))
    row_spec = pl.BlockSpec((1, LANES), lambda s, c: (0, 0))
    pad = jnp.zeros((LANES - 2 * SSD_HEADS,), F32)
    dtb = jnp.concatenate([dt_bias.reshape(-1), pad]).reshape(1, LANES)
    alog = jnp.concatenate([a_log.reshape(-1), pad]).reshape(1, LANES)
    return pl.pallas_call(
        _ssd_kernel,
        grid=(nseq, nc),
        in_specs=specs(fwd) + [dt_spec(fwd)] + specs(bwd) + [dt_spec(bwd)] + [row_spec, row_spec],
        out_specs=[
            pl.BlockSpec((CHUNK, D_INNER), lambda s, c: (fwd(s, c), 0)),
            pl.BlockSpec((CHUNK, D_INNER), lambda s, c: (bwd(s, c), 0)),
        ],
        out_shape=[jax.ShapeDtypeStruct((t, D_INNER), BF16)] * 2,
        scratch_shapes=[pltpu.VMEM((SSD_STATE, D_INNER), F32)] * 2,
        compiler_params=_cparams(("parallel", "arbitrary")),
        name="ssd",
    )(xbc, xbc, xbc, dt_raw, xbc, xbc, xbc, dt_raw, dtb, alog)


QKN_TM = 512


def _qknorm_kernel(q_ref, k_ref, gq_ref, gk_ref, qo_ref, ko_ref):
    r = lax.broadcasted_iota(jnp.int32, (LANES, LANES), 0) // NA_HEAD_DIM
    c = lax.broadcasted_iota(jnp.int32, (LANES, LANES), 1) // NA_HEAD_DIM
    same_head = jnp.where(r == c, 1.0, 0.0).astype(BF16)
    for src, g_ref, dst, scale in ((q_ref, gq_ref, qo_ref, NA_HEAD_DIM ** -0.5), (k_ref, gk_ref, ko_ref, None)):
        for p in range(NA_WIDTH // LANES):
            sl = slice(p * LANES, (p + 1) * LANES)
            x = src[:, sl].astype(F32)
            sq = x * x
            hi = sq.astype(BF16)
            lo = (sq - hi.astype(F32)).astype(BF16)
            ss = (jnp.dot(hi, same_head, preferred_element_type=F32)
                  + jnp.dot(lo, same_head, preferred_element_type=F32))
            y = x * lax.rsqrt(ss / NA_HEAD_DIM + EPS) * g_ref[...]
            if scale is not None:
                y = y * scale
            dst[:, sl] = y.astype(BF16)


def _qknorm(proj, q_norm_g, k_norm_g):
    t = proj.shape[0]
    tm = QKN_TM
    reps = LANES // NA_HEAD_DIM
    gq = jnp.tile(q_norm_g, reps).reshape(1, LANES)
    gk = jnp.tile(k_norm_g, reps).reshape(1, LANES)
    qcol = COL_QKV // NA_WIDTH
    return pl.pallas_call(
        _qknorm_kernel,
        grid=(t // tm,),
        in_specs=[
            pl.BlockSpec((tm, NA_WIDTH), lambda i: (i, qcol)),
            pl.BlockSpec((tm, NA_WIDTH), lambda i: (i, qcol + 1)),
            pl.BlockSpec((1, LANES), lambda i: (0, 0)),
            pl.BlockSpec((1, LANES), lambda i: (0, 0)),
        ],
        out_specs=[pl.BlockSpec((tm, NA_WIDTH), lambda i: (i, 0))] * 2,
        out_shape=[jax.ShapeDtypeStruct((t, NA_WIDTH), BF16)] * 2,
        compiler_params=_cparams(("parallel",)),
        name="qknorm",
    )(proj, proj, gq, gk)


def _natten_kernel(*refs):
    q_ref = refs[0]
    k_refs = refs[1:1 + WIN_ROWS]
    v_refs = refs[1 + WIN_ROWS:1 + 2 * WIN_ROWS]
    bias_ref = refs[1 + 2 * WIN_ROWS]
    o_ref = refs[2 + 2 * WIN_ROWS]
    first = lax.broadcasted_iota(jnp.int32, (GRID_W, LANES), 1) < NA_HEAD_DIM
    for p in range(NA_WIDTH // LANES):
        sl = slice(p * LANES, (p + 1) * LANES)
        qp = q_ref[:, sl].astype(F32)
        qrows = jnp.concatenate([jnp.where(first, qp, 0.0), jnp.where(first, 0.0, qp)], axis=0).astype(BF16)
        kp = jnp.concatenate([r[:, sl] for r in k_refs], axis=0)
        vp = jnp.concatenate([r[:, sl] for r in v_refs], axis=0)
        st = lax.dot_general(kp, qrows, (((1,), (1,)), ((), ())), preferred_element_type=F32)
        st = st + bias_ref[0, p]
        m = jnp.max(st, axis=0, keepdims=True)
        pe = jnp.exp(st - m)
        inv = 1.0 / jnp.sum(pe, axis=0, keepdims=True)
        pn = (pe * inv).astype(BF16)
        o2 = lax.dot_general(pn, vp, (((0,), (0,)), ((), ())), preferred_element_type=F32)
        o_ref[:, sl] = jnp.where(first, o2[:GRID_W], o2[GRID_W:]).astype(BF16)


def _na_bias_table(rpb):
    cols = jnp.arange(GRID_W)
    c0 = jnp.clip(cols - WIN_COLS // 2, 0, GRID_W - WIN_COLS)
    j = jnp.arange(GRID_W)
    valid = (j[None, :] >= c0[:, None]) & (j[None, :] < c0[:, None] + WIN_COLS)
    k = jnp.arange(2 * WIN_COLS - 1)
    onehot = (k[:, None, None] == j[None, None, :] - cols[None, :, None] + (WIN_COLS - 1)).astype(F32)
    toep = jnp.einsum("hrk,kcj->hrcj", rpb.astype(F32), onehot, precision=lax.Precision.HIGHEST)
    toep = jnp.where(valid[None, None], toep, -jnp.inf)
    per_delta = jnp.stack([toep[:, WIN_ROWS - 1 - d:2 * WIN_ROWS - 1 - d] for d in range(WIN_ROWS)], axis=0)
    hp = LANES // NA_HEAD_DIM
    t6 = per_delta.reshape(WIN_ROWS, NA_HEADS // hp, hp, WIN_ROWS, GRID_W, GRID_W)
    t6 = jnp.transpose(t6, (0, 1, 3, 5, 2, 4))
    return t6.reshape(WIN_ROWS, NA_HEADS // hp, WIN_ROWS * GRID_W, hp * GRID_W)


def _natten(qn, kn, proj, rpb, nseq, seqlen):
    t = qn.shape[0]
    rows = seqlen // GRID_W
    assert rows >= WIN_ROWS
    r0 = lambda r: jnp.clip(r - WIN_ROWS // 2, 0, rows - WIN_ROWS)
    vcol = COL_QKV // NA_WIDTH + 2
    tbl = _na_bias_table(rpb)

    def win_spec(i, colblk):
        return pl.BlockSpec((GRID_W, NA_WIDTH), lambda s, r: (s * rows + r0(r) + i, colblk))

    return pl.pallas_call(
        _natten_kernel,
        grid=(nseq, rows),
        in_specs=([pl.BlockSpec((GRID_W, NA_WIDTH), lambda s, r: (s * rows + r, 0))]
                  + [win_spec(i, 0) for i in range(WIN_ROWS)]
                  + [win_spec(i, vcol) for i in range(WIN_ROWS)]
                  + [pl.BlockSpec((1,) + tbl.shape[1:], lambda s, r: (r - r0(r), 0, 0, 0))]),
        out_specs=pl.BlockSpec((GRID_W, NA_WIDTH), lambda s, r: (s * rows + r, 0)),
        out_shape=jax.ShapeDtypeStruct((t, NA_WIDTH), BF16),
        compiler_params=_cparams(("parallel", "arbitrary")),
        name="natten",
    )(qn, *([kn] * WIN_ROWS), *([proj] * WIN_ROWS), tbl)


OUTPROJ_TM = 512


def _outproj_kernel(yf_ref, yb_ref, xs_ref, z_ref, g1_ref, g2_ref, yna_ref, x_ref, w1_ref, w2_ref,
                    dsk_ref, ng_ref, gffn_ref, x1_ref, t_ref):
    y = yf_ref[...].astype(F32) + yb_ref[...].astype(F32) + dsk_ref[...] * xs_ref[...].astype(F32)
    y = y * jax.nn.silu(z_ref[...].astype(F32))
    yn = _rms(y, ng_ref[...]).astype(BF16)
    br_ssd = jnp.dot(yn, w1_ref[...], preferred_element_type=F32)
    br_na = jnp.dot(yna_ref[...], w2_ref[...], preferred_element_type=F32)
    mix = (jax.nn.sigmoid(g1_ref[...].astype(F32)) * br_ssd
           + jax.nn.sigmoid(g2_ref[...].astype(F32)) * br_na)
    x1 = x_ref[...] + mix
    x1_ref[...] = x1
    t_ref[...] = _rms(x1, gffn_ref[...]).astype(BF16)


def _outproj(yf, yb, xbc, proj, yna, x2d, w_out, d_skip, ssd_norm_g, g_ffn):
    t = x2d.shape[0]
    tm = OUTPROJ_TM
    tok = lambda w, cb: pl.BlockSpec((tm, w), lambda i: (i, cb))
    row = lambda w: pl.BlockSpec((1, w), lambda i: (0, 0))
    dsk = jnp.repeat(d_skip, SSD_HEAD_DIM).reshape(1, D_INNER)
    return pl.pallas_call(
        _outproj_kernel,
        grid=(t // tm,),
        in_specs=[
            tok(D_INNER, 0), tok(D_INNER, 0), tok(D_INNER, 0),
            tok(D_INNER, COL_Z // D_INNER),
            tok(D_MODEL, COL_GATE // D_MODEL), tok(D_MODEL, COL_GATE // D_MODEL + 1),
            tok(NA_WIDTH, 0), tok(D_MODEL, 0),
            pl.BlockSpec((D_INNER, D_MODEL), lambda i: (0, 0)),
            pl.BlockSpec((NA_WIDTH, D_MODEL), lambda i: (D_INNER // NA_WIDTH, 0)),
            row(D_INNER), row(D_INNER), row(D_MODEL),
        ],
        out_specs=[tok(D_MODEL, 0), tok(D_MODEL, 0)],
        out_shape=[jax.ShapeDtypeStruct((t, D_MODEL), F32), jax.ShapeDtypeStruct((t, D_MODEL), BF16)],
        compiler_params=_cparams(("parallel",)),
        name="outproj",
    )(yf, yb, xbc, proj, proj, proj, yna, x2d, w_out, w_out, dsk,
      ssd_norm_g.reshape(1, D_INNER), g_ffn.reshape(1, D_MODEL))


ROUTE_TR = 256
NEG_INF = float("-inf")
RANK_NONE = 64.0


def _top_values(s, k, out_ref=None, want_rank=False):
    mx = None
    rank = jnp.full(s.shape, RANK_NONE, F32) if want_rank else None
    for r in range(k):
        mx = jnp.max(s, axis=0, keepdims=True)
        if out_ref is not None:
            out_ref[r:r + 1, :] = mx
        hit = s == mx
        if want_rank:
            rank = jnp.where(hit, float(r), rank)
        s = jnp.where(hit, NEG_INF, s)
    return mx, rank


def _route_kernel(t_ref, wpq_ref, sk_ref, rank2_ref, b_ref, cnt_ref, a_ref, top1, top2):
    half = PEER_QUERY_DIM // 2
    sub = 8
    row = lax.broadcasted_iota(jnp.int32, (sub, t_ref.shape[0]), 0)
    q = jnp.dot(t_ref[...], wpq_ref[...], preferred_element_type=F32).astype(BF16)
    for h in range(PEER_HEADS):
        s = []
        for side in range(2):
            qb = q[:, (2 * h + side) * half:(2 * h + side + 1) * half]
            s.append(lax.dot_general(sk_ref[side], qb, (((1,), (1,)), ((), ())),
                                     preferred_element_type=F32))
        _top_values(s[0], PEER_TOPK, top1)
        _, rank2 = _top_values(s[1], PEER_TOPK, top2, want_rank=True)
        tiles = [top1[0:1, :] + top2[0:sub, :], top1[0:1, :] + top2[sub:, :]]
        for r1 in range(1, sub):
            tile = top1[r1:r1 + 1, :] + top2[0:sub, :]
            tiles.append(jnp.where(row < PEER_TOPK // (r1 + 1), tile, NEG_INF))
        tiles.append(top1[sub:, :] + top2[0:1, :])
        cand = jnp.concatenate(tiles, axis=0)
        tau, _ = _top_values(cand, PEER_TOPK)
        m1 = top1[0:1, :]
        m2 = top2[0:1, :]
        z = jnp.sum(jnp.where(cand >= tau, jnp.exp(cand - (m1 + m2)), 0.0), axis=0, keepdims=True)
        cnt = jnp.zeros_like(s[0])
        for r2 in range(PEER_TOPK):
            cnt = cnt + jnp.where((s[0] + top2[r2:r2 + 1, :]) >= tau, 1.0, 0.0)
        rank2_ref[h] = pltpu.bitcast(rank2.astype(BF16), jnp.uint32)
        b_ref[h] = pltpu.bitcast((jnp.exp(s[1] - m2) / z).astype(BF16), jnp.uint32)
        cnt_ref[h] = cnt
        a_ref[h] = jnp.exp(s[0] - m1)


def _route(tn, w_pq, sub_keys):
    t = tn.shape[0]
    tr = ROUTE_TR
    qd = PEER_HEADS * PEER_QUERY_DIM
    out_spec = pl.BlockSpec((PEER_HEADS, PEER_N_KEYS, tr), lambda i: (0, 0, i))
    key_shape = (PEER_HEADS, PEER_N_KEYS, t)
    pair_spec = pl.BlockSpec((PEER_HEADS, PEER_N_KEYS // 2, tr), lambda i: (0, 0, i))
    pair_shape = (PEER_HEADS, PEER_N_KEYS // 2, t)
    return pl.pallas_call(
        _route_kernel,
        grid=(t // tr,),
        in_specs=[
            pl.BlockSpec((tr, D_MODEL), lambda i: (i, 0)),
            pl.BlockSpec((D_MODEL, qd), lambda i: (0, 0)),
            pl.BlockSpec((2, PEER_N_KEYS, PEER_QUERY_DIM // 2), lambda i: (0, 0, 0)),
        ],
        out_specs=[pair_spec, pair_spec, out_spec, out_spec],
        out_shape=[jax.ShapeDtypeStruct(pair_shape, jnp.uint32), jax.ShapeDtypeStruct(pair_shape, jnp.uint32),
                   jax.ShapeDtypeStruct(key_shape, F32), jax.ShapeDtypeStruct(key_shape, F32)],
        scratch_shapes=[pltpu.VMEM((PEER_TOPK, tr), F32)] * 2,
        compiler_params=_cparams(("parallel",)),
        name="route",
    )(tn, w_pq, sub_keys)


PEER_TP = 512
PEER_I1 = 8
PEER_EB = PEER_I1 * PEER_N_KEYS
NT_DIMS = (((1,), (1,)), ((), ()))
GATE_ROWS = PEER_N_KEYS


def _gate_block(first_keys, st_slot, g_slot, rank2_ref, b_ref, cnt_ref, a_ref):
    rows = GATE_ROWS
    for i in first_keys:
        for part in range(PEER_N_KEYS // rows):
            r0 = part * rows
            e0 = i * PEER_N_KEYS + r0
            for ck in range(st_slot.shape[1] // LANES):
                cols = slice(ck * LANES, (ck + 1) * LANES)
                wgt = jnp.zeros((rows, LANES), BF16)
                for h in range(PEER_HEADS):
                    cnt = cnt_ref[h, i:i + 1, cols].astype(BF16)
                    a1 = a_ref[h, i:i + 1, cols].astype(BF16)
                    rank2 = pltpu.bitcast(rank2_ref[h, r0 // 2:(r0 + rows) // 2, cols], BF16)
                    b2 = pltpu.bitcast(b_ref[h, r0 // 2:(r0 + rows) // 2, cols], BF16)
                    wgt = wgt + jnp.where(rank2 < cnt, b2, jnp.zeros_like(b2)) * a1
                pre = st_slot[e0:e0 + rows, cols]
                act = (0.5 * pre * (1.0 + lax.erf(pre * (2.0 ** -0.5)))).astype(BF16)
                g_slot[e0 // 2:(e0 + rows) // 2, cols] = pltpu.bitcast(act * wgt, jnp.uint32)


def _peer_kernel(t_ref, u_ref, vt_ref, rank2_ref, b_ref, cnt_lo, a_lo, cnt_hi, a_hi, x1_ref, o_ref,
                 acc, st0, st1, gated0, gated1):
    jj = pl.program_id(1)
    eb = PEER_EB
    st = (st0, st1)
    gated = (gated0, gated1)

    @pl.when(jj == 0)
    def _():
        for ref in (acc, st0, st1, gated0, gated1):
            ref[...] = jnp.zeros_like(ref)

    for k in range(2):
        other = 1 - k
        cnt_ref, a_ref = (cnt_lo, a_lo) if k == 0 else (cnt_hi, a_hi)
        acc[...] += jnp.dot(vt_ref[:, k * eb:(k + 1) * eb], pltpu.bitcast(gated[k][...], BF16),
                            preferred_element_type=F32)
        _gate_block(range(PEER_I1), st[other], gated[other], rank2_ref, b_ref, cnt_ref, a_ref)
        st[k][...] = lax.dot_general(u_ref[k * eb:(k + 1) * eb, :], t_ref[...], NT_DIMS,
                                     preferred_element_type=F32)

    @pl.when(jj == pl.num_programs(1) - 1)
    def _():
        o_ref[...] = x1_ref[...] + acc[...].T


def _peer(tn, u_bf, vt_bf, rank2, b, cnt, a, x1):
    t = tn.shape[0]
    tp, eb = PEER_TP, PEER_EB
    nb = u_bf.shape[0] // eb
    assert nb % 2 == 0
    pair_spec = pl.BlockSpec((PEER_HEADS, PEER_N_KEYS // 2, tp), lambda i, j: (0, 0, i))
    lo_spec = pl.BlockSpec((PEER_HEADS, PEER_I1, tp), lambda i, j: (0, jnp.maximum(2 * j - 1, 0), i))
    hi_spec = pl.BlockSpec((PEER_HEADS, PEER_I1, tp), lambda i, j: (0, jnp.minimum(2 * j, nb - 1), i))
    return pl.pallas_call(
        _peer_kernel,
        grid=(t // tp, nb // 2 + 1),
        in_specs=[
            pl.BlockSpec((tp, D_MODEL), lambda i, j: (i, 0)),
            pl.BlockSpec((2 * eb, D_MODEL), lambda i, j: (jnp.minimum(j, nb // 2 - 1), 0)),
            pl.BlockSpec((D_MODEL, 2 * eb), lambda i, j: (0, jnp.maximum(j - 1, 0))),
            pair_spec, pair_spec, lo_spec, lo_spec, hi_spec, hi_spec,
            pl.BlockSpec((tp, D_MODEL), lambda i, j: (i, 0)),
        ],
        out_specs=pl.BlockSpec((tp, D_MODEL), lambda i, j: (i, 0)),
        out_shape=jax.ShapeDtypeStruct((t, D_MODEL), F32),
        scratch_shapes=[pltpu.VMEM((D_MODEL, tp), F32), pltpu.VMEM((eb, tp), F32), pltpu.VMEM((eb, tp), F32),
                        pltpu.VMEM((eb // 2, tp), jnp.uint32), pltpu.VMEM((eb // 2, tp), jnp.uint32)],
        compiler_params=_cparams(("parallel", "arbitrary")),
        name="peer",
    )(tn, u_bf, vt_bf, rank2, b, cnt, a, cnt, a, x1)


def _regroup_w_in(w_in):
    pad = jnp.zeros((D_MODEL, LANES - 2 * SSD_HEADS), w_in.dtype)
    return jnp.concatenate([w_in[:, OFF_XBC:OFF_DT], w_in[:, OFF_QKV:OFF_GATE], w_in[:, :OFF_XBC],
                            w_in[:, OFF_GATE:], w_in[:, OFF_DT:OFF_QKV], pad], axis=1).astype(BF16)


def _encoder_layer(x, g_mix, w_in, conv_w, conv_b, dt_bias, a_log, d_skip, ssd_norm_g,
                   q_norm_g, k_norm_g, rpb, w_out, g_ffn, w_pq, sub_keys, expert_u, expert_v):
    nseq, seqlen, _ = x.shape
    x2d = x.reshape(nseq * seqlen, D_MODEL)
    proj, dt_raw = _inproj(x2d, g_mix, _regroup_w_in(w_in))
    xbc = _conv(proj, conv_w, conv_b, nseq, seqlen)
    yf, yb = _ssd(xbc, dt_raw, dt_bias, a_log, nseq, seqlen)
    qn, kn = _qknorm(proj, q_norm_g, k_norm_g)
    yna = _natten(qn, kn, proj, rpb, nseq, seqlen)
    x1, tn = _outproj(yf, yb, xbc, proj, yna, x2d, w_out.astype(BF16), d_skip, ssd_norm_g, g_ffn)
    rank2, b, cnt, a = _route(tn, w_pq.astype(BF16), sub_keys.astype(BF16))
    y = _peer(tn, expert_u.astype(BF16), expert_v.T.astype(BF16), rank2, b, cnt, a, x1)
    return y.reshape(nseq, seqlen, D_MODEL)


def kernel(x_prompt, x_sample, g_mix, w_in, conv_w, conv_b, dt_bias, a_log, d_skip, ssd_norm_g, q_norm_g,
           k_norm_g, rpb, w_out, g_ffn, w_pq, sub_keys, expert_u, expert_v):
    assert x_prompt.shape[1:] == x_sample.shape[1:]
    x = jnp.concatenate([x_prompt, x_sample], axis=0)
    for i in range(g_mix.shape[0]):
        x = _encoder_layer(x, g_mix[i], w_in[i], conv_w[i], conv_b[i], dt_bias[i], a_log[i], d_skip[i],
                           ssd_norm_g[i], q_norm_g[i], k_norm_g[i], rpb[i], w_out[i], g_ffn[i], w_pq[i],
                           sub_keys[i], expert_u[i], expert_v[i])
    nb = x_prompt.shape[0]
    return (x[:nb], x[nb:])
```

```python
import functools

import jax
import jax.numpy as jnp
from jax import lax
from jax.experimental import pallas as pl
from jax.experimental.pallas import tpu as pltpu

F32 = jnp.float32
BF16 = jnp.bfloat16

D_MODEL = 1024
GRID_W = 64
EPS = 1e-6
D_INNER = 2048
SSD_HEAD_DIM = 64
SSD_HEADS = 32
SSD_GROUPS = 4
SSD_STATE = 128
CONV_W = 5
CHUNK = 128
CONV_DIM = D_INNER + 2 * SSD_GROUPS * SSD_STATE
NA_HEADS = 16
NA_HEAD_DIM = 64
NA_WIDTH = 1024
WIN_ROWS = 8
WIN_COLS = 16
PEER_HEADS = 8
PEER_N_KEYS = 128
PEER_TOPK = 16
PEER_QUERY_DIM = 256
OFF_XBC = D_INNER
OFF_DT = OFF_XBC + CONV_DIM
OFF_QKV = OFF_DT + 2 * SSD_HEADS
OFF_GATE = OFF_QKV + 3 * NA_WIDTH

LANES = 128
PROJ_COLS = CONV_DIM + 3 * NA_WIDTH + D_INNER + 2 * D_MODEL + LANES
COL_QKV = CONV_DIM
COL_Z = COL_QKV + 3 * NA_WIDTH
COL_GATE = COL_Z + D_INNER
COL_DT = COL_GATE + 2 * D_MODEL

VMEM_LIMIT = 56 * 1024 * 1024


def _cparams(sem, flags=None):
    return pltpu.CompilerParams(dimension_semantics=sem, vmem_limit_bytes=VMEM_LIMIT, flags=flags)


def _rms(x, g):
    return x * lax.rsqrt(jnp.mean(x * x, axis=-1, keepdims=True) + EPS) * g


INPROJ_TM = 1024
INPROJ_TN = 1152


def _cat_specs(tm, width, na, grid_rank):
    if grid_rank == 1:
        return [pl.BlockSpec((tm, width), lambda i: (jnp.minimum(i, na - 1), 0)),
                pl.BlockSpec((tm, width), lambda i: (jnp.maximum(i - na, 0), 0))]
    return [pl.BlockSpec((tm, width), lambda i, j: (jnp.minimum(i, na - 1), 0)),
            pl.BlockSpec((tm, width), lambda i, j: (jnp.maximum(i - na, 0), 0))]


def _cat_tile(na, a_ref, b_ref):
    return jnp.where(pl.program_id(0) < na, a_ref[...], b_ref[...])


def _inproj_kernel(na, xa_ref, xb_ref, g_ref, w_ref, o_ref, dt_ref, h_scr):
    j = pl.program_id(1)

    @pl.when(j == 0)
    def _():
        h_scr[...] = _rms(_cat_tile(na, xa_ref, xb_ref), g_ref[...]).astype(BF16)

    acc = jnp.dot(h_scr[...], w_ref[...], preferred_element_type=F32)
    o_ref[...] = acc.astype(BF16)

    @pl.when(j == pl.num_programs(1) - 1)
    def _():
        dt_ref[...] = acc[:, INPROJ_TN - LANES:]


def _inproj(xa, xb, g_mix, w_cat):
    t = xa.shape[0] + xb.shape[0]
    tm, tn = INPROJ_TM, INPROJ_TN
    assert xa.shape[0] % tm == 0 and xb.shape[0] % tm == 0 and PROJ_COLS % tn == 0 and COL_DT == PROJ_COLS - LANES
    na = xa.shape[0] // tm
    return pl.pallas_call(
        functools.partial(_inproj_kernel, na),
        grid=(t // tm, PROJ_COLS // tn),
        in_specs=_cat_specs(tm, D_MODEL, na, 2) + [
            pl.BlockSpec((1, D_MODEL), lambda i, j: (0, 0)),
            pl.BlockSpec((D_MODEL, tn), lambda i, j: (0, j)),
        ],
        out_specs=[
            pl.BlockSpec((tm, tn), lambda i, j: (i, j)),
            pl.BlockSpec((tm, LANES), lambda i, j: (i, 0)),
        ],
        out_shape=[
            jax.ShapeDtypeStruct((t, PROJ_COLS), BF16),
            jax.ShapeDtypeStruct((t, LANES), F32),
        ],
        scratch_shapes=[pltpu.VMEM((tm, D_MODEL), BF16)],
        compiler_params=_cparams(("parallel", "arbitrary")),
        name="inproj",
    )(xa, xb, g_mix.reshape(1, D_MODEL), w_cat)


CONV_TL = 512
CONV_CB = 1024
HALO = 16
PAD = 8


def _conv_kernel(cur_ref, prev_ref, next_ref, w_ref, b_ref, o_ref, ext):
    i = pl.program_id(1)
    tl = cur_ref.shape[0]
    prev = prev_ref[...].astype(F32)[HALO - PAD:, :]
    nxt = next_ref[...].astype(F32)[:PAD, :]
    ext[0:PAD, :] = jnp.where(i > 0, prev, 0.0)
    ext[PAD:PAD + tl, :] = cur_ref[...].astype(F32)
    ext[PAD + tl:, :] = jnp.where(i < pl.num_programs(1) - 1, nxt, 0.0)
    acc = jnp.broadcast_to(b_ref[...], o_ref.shape)
    for k in range(CONV_W):
        acc = acc + w_ref[k:k + 1, :] * ext[pl.ds(PAD - CONV_W // 2 + k, tl), :]
    o_ref[...] = (acc * jax.nn.sigmoid(acc)).astype(BF16)


def _conv(proj, conv_w, conv_b, nseq, seqlen):
    t = proj.shape[0]
    tl, cb = CONV_TL, CONV_CB
    nl = seqlen // tl
    assert seqlen % tl == 0 and CONV_DIM % cb == 0
    return pl.pallas_call(
        _conv_kernel,
        grid=(nseq, nl, CONV_DIM // cb),
        in_specs=[
            pl.BlockSpec((tl, cb), lambda s, i, c: (s * nl + i, c)),
            pl.BlockSpec((HALO, cb), lambda s, i, c: (jnp.maximum((s * nl + i) * (tl // HALO) - 1, 0), c)),
            pl.BlockSpec((HALO, cb),
                         lambda s, i, c: (jnp.minimum((s * nl + i + 1) * (tl // HALO), t // HALO - 1), c)),
            pl.BlockSpec((CONV_W, cb), lambda s, i, c: (0, c)),
            pl.BlockSpec((1, cb), lambda s, i, c: (0, c)),
        ],
        out_specs=pl.BlockSpec((tl, cb), lambda s, i, c: (s * nl + i, c)),
        out_shape=jax.ShapeDtypeStruct((t, CONV_DIM), BF16),
        scratch_shapes=[pltpu.VMEM((tl + 2 * PAD, cb), F32)],
        compiler_params=_cparams(("parallel", "parallel", "parallel")),
        name="conv",
    )(proj, proj, proj, conv_w, conv_b.reshape(1, CONV_DIM))


def _cumsum_rows(x):
    row = lax.broadcasted_iota(jnp.int32, x.shape, 0)
    d = 1
    while d < x.shape[0]:
        x = x + jnp.where(row >= d, pltpu.roll(x, d, 0), 0.0)
        d *= 2
    return x


def _ssd_direction(reverse, xs_ref, b_ref, c_ref, dt_ref, dtb_ref, alog_ref, y_ref, st):
    row = lax.broadcasted_iota(jnp.int32, (CHUNK, CHUNK), 0)
    col = lax.broadcasted_iota(jnp.int32, (CHUNK, CHUNK), 1)
    lane1 = lax.broadcasted_iota(jnp.int32, (1, LANES), 1)
    dt_pre = dt_ref[...] + dtb_ref[...]
    dt = jnp.maximum(dt_pre, 0.0) + jnp.log1p(jnp.exp(-jnp.abs(dt_pre)))
    dta = dt * (-jnp.exp(alog_ref[...]))
    cs = _cumsum_rows(dta)
    total = cs[CHUNK - 1:CHUNK, :]
    if reverse:
        cum = total - cs + dta
        wexp = cs - dta
        tri = col >= row
    else:
        cum = cs
        wexp = total - cs
        tri = row >= col
    w = jnp.exp(wexp) * dt
    cum_t = cum.T
    dt_t = dt.T
    w_t = w.T
    dec = jnp.exp(total)
    base = SSD_HEADS if reverse else 0
    pairs_per_group = SSD_HEADS // SSD_GROUPS // 2
    for g in range(SSD_GROUPS):
        bg = b_ref[:, g * SSD_STATE:(g + 1) * SSD_STATE]
        cg = c_ref[:, g * SSD_STATE:(g + 1) * SSD_STATE]
        gmat = lax.dot_general(cg, bg, (((1,), (1,)), ((), ())), preferred_element_type=F32)
        bg_t = bg.astype(F32).T
        cg32 = cg.astype(F32)
        for hp in range(pairs_per_group):
            j = g * pairs_per_group + hp
            sl = slice(j * LANES, (j + 1) * LANES)
            xs_pair = xs_ref[:, sl].astype(F32)
            st_pair = st[:, sl]
            y_acc = jnp.zeros((CHUNK, LANES), F32)
            s_acc = jnp.zeros((SSD_STATE, LANES), F32)
            dec_pair = jnp.zeros((1, LANES), F32)
            for e in range(2):
                ln = base + 2 * j + e
                in_head = (col // SSD_HEAD_DIM) == e
                colb = jnp.broadcast_to(cum[:, ln:ln + 1], (CHUNK, CHUNK))
                decay = jnp.where(tri, jnp.exp(colb - cum_t[ln:ln + 1, :]), 0.0)
                m = gmat * decay * dt_t[ln:ln + 1, :]
                ce = cg32 * jnp.exp(colb)
                lhs = jnp.concatenate([m, ce], axis=1).astype(BF16)
                xs_m = jnp.where(in_head, xs_pair, 0.0).astype(BF16)
                st_m = jnp.where(in_head, st_pair, 0.0).astype(BF16)
                rhs = jnp.concatenate([xs_m, st_m], axis=0)
                y_acc = y_acc + jnp.dot(lhs, rhs, preferred_element_type=F32)
                btw = (bg_t * w_t[ln:ln + 1, :]).astype(BF16)
                s_acc = s_acc + jnp.dot(btw, xs_m, preferred_element_type=F32)
                dec_pair = jnp.where((lane1 // SSD_HEAD_DIM) == e,
                                     jnp.broadcast_to(dec[:, ln:ln + 1], (1, LANES)), dec_pair)
            y_ref[:, sl] = y_acc.astype(BF16)
            st[:, sl] = st_pair * dec_pair + s_acc


def _ssd_kernel(xs_f, b_f, c_f, dt_f, xs_b, b_b, c_b, dt_b, dtb_ref, alog_ref, yf_ref, yb_ref, st_f, st_b):
    @pl.when(pl.program_id(1) == 0)
    def _():
        st_f[...] = jnp.zeros_like(st_f)
        st_b[...] = jnp.zeros_like(st_b)

    _ssd_direction(False, xs_f, b_f, c_f, dt_f, dtb_ref, alog_ref, yf_ref, st_f)
    _ssd_direction(True, xs_b, b_b, c_b, dt_b, dtb_ref, alog_ref, yb_ref, st_b)


def _ssd(xbc, dt_raw, dt_bias, a_log, nseq, seqlen):
    t = xbc.shape[0]
    nc = seqlen // CHUNK
    bn = SSD_GROUPS * SSD_STATE
    fwd = lambda s, c: s * nc + c
    bwd = lambda s, c: s * nc + (nc - 1 - c)

    def specs(rb):
        return [
            pl.BlockSpec((CHUNK, D_INNER), lambda s, c: (rb(s, c), 0)),
            pl.BlockSpec((CHUNK, bn), lambda s, c: (rb(s, c), D_INNER // bn)),
            pl.BlockSpec((CHUNK, bn), lambda s, c: (rb(s, c), D_INNER // bn + 1)),
        ]

    dt_spec = lambda rb: pl.BlockSpec((CHUNK, LANES), lambda s, c: (rb(s, c), 0))
    row_spec = pl.BlockSpec((1, LANES), lambda s, c: (0, 0))
    pad = jnp.zeros((LANES - 2 * SSD_HEADS,), F32)
    dtb = jnp.concatenate([dt_bias.reshape(-1), pad]).reshape(1, LANES)
    alog = jnp.concatenate([a_log.reshape(-1), pad]).reshape(1, LANES)
    return pl.pallas_call(
        _ssd_kernel,
        grid=(nseq, nc),
        in_specs=specs(fwd) + [dt_spec(fwd)] + specs(bwd) + [dt_spec(bwd)] + [row_spec, row_spec],
        out_specs=[
            pl.BlockSpec((CHUNK, D_INNER), lambda s, c: (fwd(s, c), 0)),
            pl.BlockSpec((CHUNK, D_INNER), lambda s, c: (bwd(s, c), 0)),
        ],
        out_shape=[jax.ShapeDtypeStruct((t, D_INNER), BF16)] * 2,
        scratch_shapes=[pltpu.VMEM((SSD_STATE, D_INNER), F32)] * 2,
        compiler_params=_cparams(("parallel", "arbitrary")),
        name="ssd",
    )(xbc, xbc, xbc, dt_raw, xbc, xbc, xbc, dt_raw, dtb, alog)


QKN_TM = 512


def _qknorm_kernel(q_ref, k_ref, gq_ref, gk_ref, qo_ref, ko_ref):
    r = lax.broadcasted_iota(jnp.int32, (LANES, LANES), 0) // NA_HEAD_DIM
    c = lax.broadcasted_iota(jnp.int32, (LANES, LANES), 1) // NA_HEAD_DIM
    same_head = jnp.where(r == c, 1.0, 0.0).astype(BF16)
    for src, g_ref, dst, scale in ((q_ref, gq_ref, qo_ref, NA_HEAD_DIM ** -0.5), (k_ref, gk_ref, ko_ref, None)):
        for p in range(NA_WIDTH // LANES):
            sl = slice(p * LANES, (p + 1) * LANES)
            x = src[:, sl].astype(F32)
            sq = x * x
            hi = sq.astype(BF16)
            lo = (sq - hi.astype(F32)).astype(BF16)
            ss = (jnp.dot(hi, same_head, preferred_element_type=F32)
                  + jnp.dot(lo, same_head, preferred_element_type=F32))
            y = x * lax.rsqrt(ss / NA_HEAD_DIM + EPS) * g_ref[...]
            if scale is not None:
                y = y * scale
            dst[:, sl] = y.astype(BF16)


def _qknorm(proj, q_norm_g, k_norm_g):
    t = proj.shape[0]
    tm = QKN_TM
    reps = LANES // NA_HEAD_DIM
    gq = jnp.tile(q_norm_g, reps).reshape(1, LANES)
    gk = jnp.tile(k_norm_g, reps).reshape(1, LANES)
    qcol = COL_QKV // NA_WIDTH
    return pl.pallas_call(
        _qknorm_kernel,
        grid=(t // tm,),
        in_specs=[
            pl.BlockSpec((tm, NA_WIDTH), lambda i: (i, qcol)),
            pl.BlockSpec((tm, NA_WIDTH), lambda i: (i, qcol + 1)),
            pl.BlockSpec((1, LANES), lambda i: (0, 0)),
            pl.BlockSpec((1, LANES), lambda i: (0, 0)),
        ],
        out_specs=[pl.BlockSpec((tm, NA_WIDTH), lambda i: (i, 0))] * 2,
        out_shape=[jax.ShapeDtypeStruct((t, NA_WIDTH), BF16)] * 2,
        compiler_params=_cparams(("parallel",)),
        name="qknorm",
    )(proj, proj, gq, gk)


def _natten_kernel(*refs):
    q_ref = refs[0]
    k_refs = refs[1:1 + WIN_ROWS]
    v_refs = refs[1 + WIN_ROWS:1 + 2 * WIN_ROWS]
    bias_ref = refs[1 + 2 * WIN_ROWS]
    o_ref = refs[2 + 2 * WIN_ROWS]
    first = lax.broadcasted_iota(jnp.int32, (GRID_W, LANES), 1) < NA_HEAD_DIM
    for p in range(NA_WIDTH // LANES):
        sl = slice(p * LANES, (p + 1) * LANES)
        qp = q_ref[:, sl].astype(F32)
        qrows = jnp.concatenate([jnp.where(first, qp, 0.0), jnp.where(first, 0.0, qp)], axis=0).astype(BF16)
        kp = jnp.concatenate([r[:, sl] for r in k_refs], axis=0)
        vp = jnp.concatenate([r[:, sl] for r in v_refs], axis=0)
        st = lax.dot_general(kp, qrows, (((1,), (1,)), ((), ())), preferred_element_type=F32)
        st = st + bias_ref[0, p]
        m = jnp.max(st, axis=0, keepdims=True)
        pe = jnp.exp(st - m)
        inv = 1.0 / jnp.sum(pe, axis=0, keepdims=True)
        pn = (pe * inv).astype(BF16)
        o2 = lax.dot_general(pn, vp, (((0,), (0,)), ((), ())), preferred_element_type=F32)
        o_ref[:, sl] = jnp.where(first, o2[:GRID_W], o2[GRID_W:]).astype(BF16)


def _na_bias_table(rpb):
    cols = jnp.arange(GRID_W)
    c0 = jnp.clip(cols - WIN_COLS // 2, 0, GRID_W - WIN_COLS)
    j = jnp.arange(GRID_W)
    valid = (j[None, :] >= c0[:, None]) & (j[None, :] < c0[:, None] + WIN_COLS)
    k = jnp.arange(2 * WIN_COLS - 1)
    onehot = (k[:, None, None] == j[None, None, :] - cols[None, :, None] + (WIN_COLS - 1)).astype(F32)
    toep = jnp.einsum("hrk,kcj->hrcj", rpb.astype(F32), onehot, precision=lax.Precision.HIGHEST)
    toep = jnp.where(valid[None, None], toep, -jnp.inf)
    per_delta = jnp.stack([toep[:, WIN_ROWS - 1 - d:2 * WIN_ROWS - 1 - d] for d in range(WIN_ROWS)], axis=0)
    hp = LANES // NA_HEAD_DIM
    t6 = per_delta.reshape(WIN_ROWS, NA_HEADS // hp, hp, WIN_ROWS, GRID_W, GRID_W)
    t6 = jnp.transpose(t6, (0, 1, 3, 5, 2, 4))
    return t6.reshape(WIN_ROWS, NA_HEADS // hp, WIN_ROWS * GRID_W, hp * GRID_W)


def _natten(qn, kn, proj, rpb, nseq, seqlen):
    t = qn.shape[0]
    rows = seqlen // GRID_W
    assert rows >= WIN_ROWS
    r0 = lambda r: jnp.clip(r - WIN_ROWS // 2, 0, rows - WIN_ROWS)
    vcol = COL_QKV // NA_WIDTH + 2
    tbl = _na_bias_table(rpb)

    def win_spec(i, colblk):
        return pl.BlockSpec((GRID_W, NA_WIDTH), lambda s, r: (s * rows + r0(r) + i, colblk))

    return pl.pallas_call(
        _natten_kernel,
        grid=(nseq, rows),
        in_specs=([pl.BlockSpec((GRID_W, NA_WIDTH), lambda s, r: (s * rows + r, 0))]
                  + [win_spec(i, 0) for i in range(WIN_ROWS)]
                  + [win_spec(i, vcol) for i in range(WIN_ROWS)]
                  + [pl.BlockSpec((1,) + tbl.shape[1:], lambda s, r: (r - r0(r), 0, 0, 0))]),
        out_specs=pl.BlockSpec((GRID_W, NA_WIDTH), lambda s, r: (s * rows + r, 0)),
        out_shape=jax.ShapeDtypeStruct((t, NA_WIDTH), BF16),
        compiler_params=_cparams(("parallel", "arbitrary")),
        name="natten",
    )(qn, *([kn] * WIN_ROWS), *([proj] * WIN_ROWS), tbl)


OUTPROJ_TM = 512


def _outproj_kernel(na, yf_ref, yb_ref, xs_ref, z_ref, g1_ref, g2_ref, yna_ref, xa_ref, xb_ref, w1_ref, w2_ref,
                    dsk_ref, ng_ref, gffn_ref, x1_ref, t_ref):
    y = yf_ref[...].astype(F32) + yb_ref[...].astype(F32) + dsk_ref[...] * xs_ref[...].astype(F32)
    y = y * jax.nn.silu(z_ref[...].astype(F32))
    yn = _rms(y, ng_ref[...]).astype(BF16)
    br_ssd = jnp.dot(yn, w1_ref[...], preferred_element_type=F32)
    br_na = jnp.dot(yna_ref[...], w2_ref[...], preferred_element_type=F32)
    mix = (jax.nn.sigmoid(g1_ref[...].astype(F32)) * br_ssd
           + jax.nn.sigmoid(g2_ref[...].astype(F32)) * br_na)
    x1 = _cat_tile(na, xa_ref, xb_ref) + mix
    x1_ref[...] = x1
    t_ref[...] = pltpu.bitcast(_rms(x1, gffn_ref[...]).astype(BF16), jnp.uint32)


def _outproj(yf, yb, xbc, proj, yna, xa, xb, w_out, d_skip, ssd_norm_g, g_ffn):
    t = xa.shape[0] + xb.shape[0]
    tm = OUTPROJ_TM
    na = xa.shape[0] // tm
    tok = lambda w, cb: pl.BlockSpec((tm, w), lambda i: (i, cb))
    row = lambda w: pl.BlockSpec((1, w), lambda i: (0, 0))
    dsk = jnp.repeat(d_skip, SSD_HEAD_DIM).reshape(1, D_INNER)
    return pl.pallas_call(
        functools.partial(_outproj_kernel, na),
        grid=(t // tm,),
        in_specs=[
            tok(D_INNER, 0), tok(D_INNER, 0), tok(D_INNER, 0),
            tok(D_INNER, COL_Z // D_INNER),
            tok(D_MODEL, COL_GATE // D_MODEL), tok(D_MODEL, COL_GATE // D_MODEL + 1),
            tok(NA_WIDTH, 0), *_cat_specs(tm, D_MODEL, na, 1),
            pl.BlockSpec((D_INNER, D_MODEL), lambda i: (0, 0)),
            pl.BlockSpec((NA_WIDTH, D_MODEL), lambda i: (D_INNER // NA_WIDTH, 0)),
            row(D_INNER), row(D_INNER), row(D_MODEL),
        ],
        out_specs=[tok(D_MODEL, 0), pl.BlockSpec((tm // 2, D_MODEL), lambda i: (i, 0))],
        out_shape=[jax.ShapeDtypeStruct((t, D_MODEL), F32), jax.ShapeDtypeStruct((t // 2, D_MODEL), jnp.uint32)],
        compiler_params=_cparams(("parallel",)),
        name="outproj",
    )(yf, yb, xbc, proj, proj, proj, yna, xa, xb, w_out, w_out, dsk,
      ssd_norm_g.reshape(1, D_INNER), g_ffn.reshape(1, D_MODEL))


ROUTE_TR = 256
NEG_INF = float("-inf")
RANK_NONE = 64.0


def _top_values(ss, k, out_refs=None, want_rank=None):
    n = len(ss)
    ss = list(ss)
    out_refs = out_refs or [None] * n
    want_rank = want_rank or [False] * n
    ranks = [jnp.full(s.shape, RANK_NONE, F32) if w else None for s, w in zip(ss, want_rank)]
    mx = [None] * n
    for r in range(k):
        for i in range(n):
            mx[i] = jnp.max(ss[i], axis=0, keepdims=True)
            if out_refs[i] is not None:
                out_refs[i][r:r + 1, :] = mx[i]
            hit = ss[i] == mx[i]
            if ranks[i] is not None:
                ranks[i] = jnp.where(hit, float(r), ranks[i])
            ss[i] = jnp.where(hit, NEG_INF, ss[i])
    return mx, ranks


ROUTE_HEAD_GROUP = 4


def _route_kernel(t_ref, wpq_ref, sk_ref, rank2_ref, b_ref, cnt_ref, a_ref, top1, top2):
    half = PEER_QUERY_DIM // 2
    sub = 8
    tokens = pltpu.bitcast(t_ref[...], BF16)
    row = lax.broadcasted_iota(jnp.int32, (sub, tokens.shape[0]), 0)
    q = jnp.dot(tokens, wpq_ref[...], preferred_element_type=F32).astype(BF16)
    for h0 in range(0, PEER_HEADS, ROUTE_HEAD_GROUP):
        heads = range(h0, h0 + ROUTE_HEAD_GROUP)
        s1, s2 = [], []
        for h in heads:
            for side, dst in ((0, s1), (1, s2)):
                qb = q[:, (2 * h + side) * half:(2 * h + side + 1) * half]
                dst.append(lax.dot_general(sk_ref[side], qb, (((1,), (1,)), ((), ())),
                                           preferred_element_type=F32))
        _, ranks = _top_values(s1 + s2, PEER_TOPK, [top1.at[h] for h in heads] + [top2.at[h] for h in heads],
                               [False] * len(s1) + [True] * len(s2))
        rank2 = ranks[len(s1):]
        cands = []
        for h in heads:
            t1, t2 = top1.at[h], top2.at[h]
            tiles = [t1[0:1, :] + t2[0:sub, :], t1[0:1, :] + t2[sub:, :]]
            for r1 in range(1, sub):
                tile = t1[r1:r1 + 1, :] + t2[0:sub, :]
                tiles.append(jnp.where(row < PEER_TOPK // (r1 + 1), tile, NEG_INF))
            tiles.append(t1[sub:, :] + t2[0:1, :])
            cands.append(jnp.concatenate(tiles, axis=0))
        taus, _ = _top_values(cands, PEER_TOPK)
        for g, h in enumerate(heads):
            t1, t2 = top1.at[h], top2.at[h]
            cand, tau = cands[g], taus[g]
            m1 = t1[0:1, :]
            m2 = t2[0:1, :]
            z = jnp.sum(jnp.where(cand >= tau, jnp.exp(cand - (m1 + m2)), 0.0), axis=0, keepdims=True)
            cnt = jnp.zeros_like(s1[g])
            for r2 in range(PEER_TOPK):
                cnt = cnt + jnp.where((s1[g] + t2[r2:r2 + 1, :]) >= tau, 1.0, 0.0)
            rank2_ref[h] = pltpu.bitcast(rank2[g].astype(BF16), jnp.uint32)
            b_ref[h] = pltpu.bitcast((jnp.exp(s2[g] - m2) / z).astype(BF16), jnp.uint32)
            cnt_ref[h] = cnt
            a_ref[h] = jnp.exp(s1[g] - m1)


def _route(tn, w_pq, sub_keys):
    t = 2 * tn.shape[0]
    tr = ROUTE_TR
    qd = PEER_HEADS * PEER_QUERY_DIM
    out_spec = pl.BlockSpec((PEER_HEADS, PEER_N_KEYS, tr), lambda i: (0, 0, i))
    key_shape = (PEER_HEADS, PEER_N_KEYS, t)
    pair_spec = pl.BlockSpec((PEER_HEADS, PEER_N_KEYS // 2, tr), lambda i: (0, 0, i))
    pair_shape = (PEER_HEADS, PEER_N_KEYS // 2, t)
    return pl.pallas_call(
        _route_kernel,
        grid=(t // tr,),
        in_specs=[
            pl.BlockSpec((tr // 2, D_MODEL), lambda i: (i, 0)),
            pl.BlockSpec((D_MODEL, qd), lambda i: (0, 0)),
            pl.BlockSpec((2, PEER_N_KEYS, PEER_QUERY_DIM // 2), lambda i: (0, 0, 0)),
        ],
        out_specs=[pair_spec, pair_spec, out_spec, out_spec],
        out_shape=[jax.ShapeDtypeStruct(pair_shape, jnp.uint32), jax.ShapeDtypeStruct(pair_shape, jnp.uint32),
                   jax.ShapeDtypeStruct(key_shape, F32), jax.ShapeDtypeStruct(key_shape, F32)],
        scratch_shapes=[pltpu.VMEM((PEER_HEADS, PEER_TOPK, tr), F32)] * 2,
        compiler_params=_cparams(("parallel",)),
        name="route",
    )(tn, w_pq, sub_keys)


PEER_TP = 512
PEER_I1 = 8
PEER_EB = PEER_I1 * PEER_N_KEYS
NT_DIMS = (((1,), (1,)), ((), ()))
GATE_ROWS = PEER_N_KEYS


def _gate_block(first_keys, st_slot, g_slot, rank2_ref, b_ref, cnt_ref, a_ref):
    rows = GATE_ROWS
    for i in first_keys:
        for part in range(PEER_N_KEYS // rows):
            r0 = part * rows
            e0 = i * PEER_N_KEYS + r0
            for ck in range(st_slot.shape[1] // LANES):
                cols = slice(ck * LANES, (ck + 1) * LANES)
                wgt = jnp.zeros((rows, LANES), BF16)
                for h in range(PEER_HEADS):
                    cnt = cnt_ref[h, i:i + 1, cols].astype(BF16)
                    a1 = a_ref[h, i:i + 1, cols].astype(BF16)
                    rank2 = pltpu.bitcast(rank2_ref[h, r0 // 2:(r0 + rows) // 2, cols], BF16)
                    b2 = pltpu.bitcast(b_ref[h, r0 // 2:(r0 + rows) // 2, cols], BF16)
                    wgt = wgt + jnp.where(rank2 < cnt, b2, jnp.zeros_like(b2)) * a1
                pre = st_slot[e0:e0 + rows, cols].astype(BF16)
                act = 0.5 * pre * (1.0 + lax.erf(pre * (2.0 ** -0.5)))
                g_slot[e0 // 2:(e0 + rows) // 2, cols] = pltpu.bitcast(act * wgt, jnp.uint32)


def _peer_kernel(na, t_ref, u_ref, vt_ref, rank2_ref, b_ref, cnt_lo, a_lo, cnt_hi, a_hi, x1_ref, oa_ref, ob_ref,
                 acc, st0, st1, gated0, gated1):
    jj = pl.program_id(1)
    eb = PEER_EB
    st = (st0, st1)
    gated = (gated0, gated1)

    @pl.when(jj == 0)
    def _():
        for ref in (acc, st0, st1, gated0, gated1):
            ref[...] = jnp.zeros_like(ref)

    for k in range(2):
        other = 1 - k
        cnt_ref, a_ref = (cnt_lo, a_lo) if k == 0 else (cnt_hi, a_hi)
        acc[...] = jnp.dot(pltpu.bitcast(vt_ref[:, k * eb:(k + 1) * eb], BF16),
                           pltpu.bitcast(gated[k][...], BF16), preferred_element_type=F32) + acc[...]
        _gate_block(range(PEER_I1), st[other], gated[other], rank2_ref, b_ref, cnt_ref, a_ref)
        st[k][...] = lax.dot_general(pltpu.bitcast(u_ref[k * eb // 2:(k + 1) * eb // 2, :], BF16),
                                     pltpu.bitcast(t_ref[...], BF16), NT_DIMS,
                                     preferred_element_type=F32)

    last = jj == pl.num_programs(1) - 1
    for o_ref, mine in ((oa_ref, pl.program_id(0) < na), (ob_ref, pl.program_id(0) >= na)):
        @pl.when(last & mine)
        def _(o_ref=o_ref):
            o_ref[...] = x1_ref[...] + acc[...].T


def _peer(tn, u_bf, vt_bf, rank2, b, cnt, a, x1, ta):
    t = x1.shape[0]
    tp, eb = PEER_TP, PEER_EB
    assert ta % tp == 0
    na = ta // tp
    nb = 2 * u_bf.shape[0] // eb
    assert nb % 2 == 0
    pair_spec = pl.BlockSpec((PEER_HEADS, PEER_N_KEYS // 2, tp), lambda i, j: (0, 0, i))
    lo_spec = pl.BlockSpec((PEER_HEADS, PEER_I1, tp), lambda i, j: (0, jnp.maximum(2 * j - 1, 0), i))
    hi_spec = pl.BlockSpec((PEER_HEADS, PEER_I1, tp), lambda i, j: (0, jnp.minimum(2 * j, nb - 1), i))
    return pl.pallas_call(
        functools.partial(_peer_kernel, na),
        grid=(t // tp, nb // 2 + 1),
        in_specs=[
            pl.BlockSpec((tp // 2, D_MODEL), lambda i, j: (i, 0)),
            pl.BlockSpec((eb, D_MODEL), lambda i, j: (jnp.minimum(j, nb // 2 - 1), 0)),
            pl.BlockSpec((D_MODEL // 2, 2 * eb), lambda i, j: (0, jnp.maximum(j - 1, 0))),
            pair_spec, pair_spec, lo_spec, lo_spec, hi_spec, hi_spec,
            pl.BlockSpec((tp, D_MODEL), lambda i, j: (i, 0)),
        ],
        out_specs=_cat_specs(tp, D_MODEL, na, 2),
        out_shape=[jax.ShapeDtypeStruct((ta, D_MODEL), F32), jax.ShapeDtypeStruct((t - ta, D_MODEL), F32)],
        scratch_shapes=[pltpu.VMEM((D_MODEL, tp), F32), pltpu.VMEM((eb, tp), F32), pltpu.VMEM((eb, tp), F32),
                        pltpu.VMEM((eb // 2, tp), jnp.uint32), pltpu.VMEM((eb // 2, tp), jnp.uint32)],
        compiler_params=_cparams(("arbitrary", "arbitrary")),
        name="peer",
    )(tn, u_bf, vt_bf, rank2, b, cnt, a, cnt, a, x1)


def _pack_rows(x):
    n2, c = x.shape
    return lax.bitcast_convert_type(jnp.swapaxes(x.reshape(n2 // 2, 2, c), 1, 2), jnp.uint32)


def _regroup_w_in(w_in):
    pad = jnp.zeros((D_MODEL, LANES - 2 * SSD_HEADS), w_in.dtype)
    return jnp.concatenate([w_in[:, OFF_XBC:OFF_DT], w_in[:, OFF_QKV:OFF_GATE], w_in[:, :OFF_XBC],
                            w_in[:, OFF_GATE:], w_in[:, OFF_DT:OFF_QKV], pad], axis=1).astype(BF16)


def _encoder_layer(xa, xb, seqlen, g_mix, w_in, conv_w, conv_b, dt_bias, a_log, d_skip, ssd_norm_g,
                   q_norm_g, k_norm_g, rpb, w_out, g_ffn, w_pq, sub_keys, expert_u, expert_v):
    nseq = (xa.shape[0] + xb.shape[0]) // seqlen
    proj, dt_raw = _inproj(xa, xb, g_mix, _regroup_w_in(w_in))
    xbc = _conv(proj, conv_w, conv_b, nseq, seqlen)
    yf, yb = _ssd(xbc, dt_raw, dt_bias, a_log, nseq, seqlen)
    qn, kn = _qknorm(proj, q_norm_g, k_norm_g)
    yna = _natten(qn, kn, proj, rpb, nseq, seqlen)
    x1, tn = _outproj(yf, yb, xbc, proj, yna, xa, xb, w_out.astype(BF16), d_skip, ssd_norm_g, g_ffn)
    rank2, b, cnt, a = _route(tn, w_pq.astype(BF16), sub_keys.astype(BF16))
    return _peer(tn, _pack_rows(expert_u.astype(BF16)), _pack_rows(expert_v.T.astype(BF16)), rank2, b, cnt, a, x1,
                 xa.shape[0])


def kernel(x_prompt, x_sample, g_mix, w_in, conv_w, conv_b, dt_bias, a_log, d_skip, ssd_norm_g, q_norm_g,
           k_norm_g, rpb, w_out, g_ffn, w_pq, sub_keys, expert_u, expert_v):
    assert x_prompt.shape[1:] == x_sample.shape[1:]
    seqlen = x_prompt.shape[1]
    xa = x_prompt.reshape(-1, D_MODEL)
    xb = x_sample.reshape(-1, D_MODEL)
    for i in range(g_mix.shape[0]):
        xa, xb = _encoder_layer(xa, xb, seqlen, g_mix[i], w_in[i], conv_w[i], conv_b[i], dt_bias[i], a_log[i],
                                d_skip[i], ssd_norm_g[i], q_norm_g[i], k_norm_g[i], rpb[i], w_out[i], g_ffn[i],
                                w_pq[i], sub_keys[i], expert_u[i], expert_v[i])
    return (xa.reshape(x_prompt.shape), xb.reshape(x_sample.shape))
```

```python
import functools

import jax
import jax.numpy as jnp
from jax import lax
from jax.experimental import pallas as pl
from jax.experimental.pallas import tpu as pltpu

F32 = jnp.float32
BF16 = jnp.bfloat16

D_MODEL = 1024
GRID_W = 64
EPS = 1e-6
D_INNER = 2048
SSD_HEAD_DIM = 64
SSD_HEADS = 32
SSD_GROUPS = 4
SSD_STATE = 128
CONV_W = 5
CHUNK = 128
CONV_DIM = D_INNER + 2 * SSD_GROUPS * SSD_STATE
NA_HEADS = 16
NA_HEAD_DIM = 64
NA_WIDTH = 1024
WIN_ROWS = 8
WIN_COLS = 16
PEER_HEADS = 8
PEER_N_KEYS = 128
PEER_TOPK = 16
PEER_QUERY_DIM = 256
OFF_XBC = D_INNER
OFF_DT = OFF_XBC + CONV_DIM
OFF_QKV = OFF_DT + 2 * SSD_HEADS
OFF_GATE = OFF_QKV + 3 * NA_WIDTH

LANES = 128
MXU_COLS = 256
COL_QKV = CONV_DIM
COL_Z = COL_QKV + 3 * NA_WIDTH
COL_GATE = COL_Z + D_INNER
COL_DT = COL_GATE + 2 * D_MODEL
INPROJ_TN = 6 * MXU_COLS
PROJ_COLS = -(-(COL_DT + LANES) // INPROJ_TN) * INPROJ_TN
DT_LOCAL = COL_DT - (PROJ_COLS - INPROJ_TN)

VMEM_LIMIT = 56 * 1024 * 1024


def _cparams(sem, flags=None):
    return pltpu.CompilerParams(dimension_semantics=sem, vmem_limit_bytes=VMEM_LIMIT, flags=flags)


def _rms(x, g):
    return x * lax.rsqrt(jnp.mean(x * x, axis=-1, keepdims=True) + EPS) * g


INPROJ_TM = 1024


def _cat_specs(tm, width, na, grid_rank):
    if grid_rank == 1:
        return [pl.BlockSpec((tm, width), lambda i: (jnp.minimum(i, na - 1), 0)),
                pl.BlockSpec((tm, width), lambda i: (jnp.maximum(i - na, 0), 0))]
    return [pl.BlockSpec((tm, width), lambda i, j: (jnp.minimum(i, na - 1), 0)),
            pl.BlockSpec((tm, width), lambda i, j: (jnp.maximum(i - na, 0), 0))]


def _cat_tile(na, a_ref, b_ref):
    return jnp.where(pl.program_id(0) < na, a_ref[...], b_ref[...])


def _inproj_kernel(na, xa_ref, xb_ref, g_ref, w_ref, o_ref, dt_ref, h_scr):
    j = pl.program_id(1)

    @pl.when(j == 0)
    def _():
        h_scr[...] = _rms(_cat_tile(na, xa_ref, xb_ref), g_ref[...]).astype(BF16)

    acc = jnp.dot(h_scr[...], w_ref[...], preferred_element_type=F32)
    o_ref[...] = acc.astype(BF16)

    @pl.when(j == pl.num_programs(1) - 1)
    def _():
        dt_ref[...] = acc[:, DT_LOCAL:DT_LOCAL + LANES]


def _inproj(xa, xb, g_mix, w_cat):
    t = xa.shape[0] + xb.shape[0]
    tm, tn = INPROJ_TM, INPROJ_TN
    assert xa.shape[0] % tm == 0 and xb.shape[0] % tm == 0 and PROJ_COLS % tn == 0 and COL_DT >= PROJ_COLS - tn
    na = xa.shape[0] // tm
    return pl.pallas_call(
        functools.partial(_inproj_kernel, na),
        grid=(t // tm, PROJ_COLS // tn),
        in_specs=_cat_specs(tm, D_MODEL, na, 2) + [
            pl.BlockSpec((1, D_MODEL), lambda i, j: (0, 0)),
            pl.BlockSpec((D_MODEL, tn), lambda i, j: (0, j)),
        ],
        out_specs=[
            pl.BlockSpec((tm, tn), lambda i, j: (i, j)),
            pl.BlockSpec((tm, LANES), lambda i, j: (i, 0)),
        ],
        out_shape=[
            jax.ShapeDtypeStruct((t, PROJ_COLS), BF16),
            jax.ShapeDtypeStruct((t, LANES), F32),
        ],
        scratch_shapes=[pltpu.VMEM((tm, D_MODEL), BF16)],
        compiler_params=_cparams(("parallel", "arbitrary")),
        name="inproj",
    )(xa, xb, g_mix.reshape(1, D_MODEL), w_cat)


CONV_TL = 512
CONV_CB = 1024
HALO = 16
PAD = 8


def _conv_kernel(cur_ref, prev_ref, next_ref, w_ref, b_ref, o_ref, ext):
    i = pl.program_id(1)
    tl = cur_ref.shape[0]
    prev = prev_ref[...].astype(F32)[HALO - PAD:, :]
    nxt = next_ref[...].astype(F32)[:PAD, :]
    ext[0:PAD, :] = jnp.where(i > 0, prev, 0.0)
    ext[PAD:PAD + tl, :] = cur_ref[...].astype(F32)
    ext[PAD + tl:, :] = jnp.where(i < pl.num_programs(1) - 1, nxt, 0.0)
    acc = jnp.broadcast_to(b_ref[...], o_ref.shape)
    for k in range(CONV_W):
        acc = acc + w_ref[k:k + 1, :] * ext[pl.ds(PAD - CONV_W // 2 + k, tl), :]
    o_ref[...] = (acc * jax.nn.sigmoid(acc)).astype(BF16)


def _conv(proj, conv_w, conv_b, nseq, seqlen):
    t = proj.shape[0]
    tl, cb = CONV_TL, CONV_CB
    nl = seqlen // tl
    assert seqlen % tl == 0 and CONV_DIM % cb == 0
    return pl.pallas_call(
        _conv_kernel,
        grid=(nseq, nl, CONV_DIM // cb),
        in_specs=[
            pl.BlockSpec((tl, cb), lambda s, i, c: (s * nl + i, c)),
            pl.BlockSpec((HALO, cb), lambda s, i, c: (jnp.maximum((s * nl + i) * (tl // HALO) - 1, 0), c)),
            pl.BlockSpec((HALO, cb),
                         lambda s, i, c: (jnp.minimum((s * nl + i + 1) * (tl // HALO), t // HALO - 1), c)),
            pl.BlockSpec((CONV_W, cb), lambda s, i, c: (0, c)),
            pl.BlockSpec((1, cb), lambda s, i, c: (0, c)),
        ],
        out_specs=pl.BlockSpec((tl, cb), lambda s, i, c: (s * nl + i, c)),
        out_shape=jax.ShapeDtypeStruct((t, CONV_DIM), BF16),
        scratch_shapes=[pltpu.VMEM((tl + 2 * PAD, cb), F32)],
        compiler_params=_cparams(("parallel", "parallel", "parallel")),
        name="conv",
    )(proj, proj, proj, conv_w, conv_b.reshape(1, CONV_DIM))


def _cumsum_rows(x):
    row = lax.broadcasted_iota(jnp.int32, x.shape, 0)
    d = 1
    while d < x.shape[0]:
        x = x + jnp.where(row >= d, pltpu.roll(x, d, 0), 0.0)
        d *= 2
    return x


def _ssd_direction(reverse, xs_ref, b_ref, c_ref, dt_ref, dtb_ref, alog_ref, y_ref, st):
    row = lax.broadcasted_iota(jnp.int32, (CHUNK, CHUNK), 0)
    col = lax.broadcasted_iota(jnp.int32, (CHUNK, CHUNK), 1)
    lane1 = lax.broadcasted_iota(jnp.int32, (1, LANES), 1)
    dt_pre = dt_ref[...] + dtb_ref[...]
    dt = jnp.maximum(dt_pre, 0.0) + jnp.log1p(jnp.exp(-jnp.abs(dt_pre)))
    dta = dt * (-jnp.exp(alog_ref[...]))
    cs = _cumsum_rows(dta)
    total = cs[CHUNK - 1:CHUNK, :]
    if reverse:
        cum = total - cs + dta
        wexp = cs - dta
        tri = col >= row
    else:
        cum = cs
        wexp = total - cs
        tri = row >= col
    w = jnp.exp(wexp) * dt
    cum_t = cum.T
    dt_t = dt.T
    w_t = w.T
    dec = jnp.exp(total)
    base = SSD_HEADS if reverse else 0
    pairs_per_group = SSD_HEADS // SSD_GROUPS // 2
    for g in range(SSD_GROUPS):
        bg = b_ref[:, g * SSD_STATE:(g + 1) * SSD_STATE]
        cg = c_ref[:, g * SSD_STATE:(g + 1) * SSD_STATE]
        gmat = lax.dot_general(cg, bg, (((1,), (1,)), ((), ())), preferred_element_type=F32)
        bg_t = bg.astype(F32).T
        cg32 = cg.astype(F32)
        for hp in range(pairs_per_group):
            j = g * pairs_per_group + hp
            sl = slice(j * LANES, (j + 1) * LANES)
            xs_pair = xs_ref[:, sl].astype(F32)
            st_pair = st[:, sl]
            y_acc = jnp.zeros((CHUNK, LANES), F32)
            s_acc = jnp.zeros((SSD_STATE, LANES), F32)
            dec_pair = jnp.zeros((1, LANES), F32)
            for e in range(2):
                ln = base + 2 * j + e
                in_head = (col // SSD_HEAD_DIM) == e
                colb = jnp.broadcast_to(cum[:, ln:ln + 1], (CHUNK, CHUNK))
                decay = jnp.where(tri, jnp.exp(colb - cum_t[ln:ln + 1, :]), 0.0)
                m = gmat * decay * dt_t[ln:ln + 1, :]
                ce = cg32 * jnp.exp(colb)
                lhs = jnp.concatenate([m, ce], axis=1).astype(BF16)
                xs_m = jnp.where(in_head, xs_pair, 0.0).astype(BF16)
                st_m = jnp.where(in_head, st_pair, 0.0).astype(BF16)
                rhs = jnp.concatenate([xs_m, st_m], axis=0)
                y_acc = y_acc + jnp.dot(lhs, rhs, preferred_element_type=F32)
                btw = (bg_t * w_t[ln:ln + 1, :]).astype(BF16)
                s_acc = s_acc + jnp.dot(btw, xs_m, preferred_element_type=F32)
                dec_pair = jnp.where((lane1 // SSD_HEAD_DIM) == e,
                                     jnp.broadcast_to(dec[:, ln:ln + 1], (1, LANES)), dec_pair)
            y_ref[:, sl] = y_acc.astype(BF16)
            st[:, sl] = st_pair * dec_pair + s_acc


def _ssd_kernel(xs_f, b_f, c_f, dt_f, xs_b, b_b, c_b, dt_b, dtb_ref, alog_ref, yf_ref, yb_ref, st_f, st_b):
    @pl.when(pl.program_id(1) == 0)
    def _():
        st_f[...] = jnp.zeros_like(st_f)
        st_b[...] = jnp.zeros_like(st_b)

    _ssd_direction(False, xs_f, b_f, c_f, dt_f, dtb_ref, alog_ref, yf_ref, st_f)
    _ssd_direction(True, xs_b, b_b, c_b, dt_b, dtb_ref, alog_ref, yb_ref, st_b)


def _ssd(xbc, dt_raw, dt_bias, a_log, nseq, seqlen):
    t = xbc.shape[0]
    nc = seqlen // CHUNK
    bn = SSD_GROUPS * SSD_STATE
    fwd = lambda s, c: s * nc + c
    bwd = lambda s, c: s * nc + (nc - 1 - c)

    def specs(rb):
        return [
            pl.BlockSpec((CHUNK, D_INNER), lambda s, c: (rb(s, c), 0)),
            pl.BlockSpec((CHUNK, bn), lambda s, c: (rb(s, c), D_INNER // bn)),
            pl.BlockSpec((CHUNK, bn), lambda s, c: (rb(s, c), D_INNER // bn + 1)),
        ]

    dt_spec = lambda rb: pl.BlockSpec((CHUNK, LANES), lambda s, c: (rb(s, c), 0))
    row_spec = pl.BlockSpec((1, LANES), lambda s, c: (0, 0))
    pad = jnp.zeros((LANES - 2 * SSD_HEADS,), F32)
    dtb = jnp.concatenate([dt_bias.reshape(-1), pad]).reshape(1, LANES)
    alog = jnp.concatenate([a_log.reshape(-1), pad]).reshape(1, LANES)
    return pl.pallas_call(
        _ssd_kernel,
        grid=(nseq, nc),
        in_specs=specs(fwd) + [dt_spec(fwd)] + specs(bwd) + [dt_spec(bwd)] + [row_spec, row_spec],
        out_specs=[
            pl.BlockSpec((CHUNK, D_INNER), lambda s, c: (fwd(s, c), 0)),
            pl.BlockSpec((CHUNK, D_INNER), lambda s, c: (bwd(s, c), 0)),
        ],
        out_shape=[jax.ShapeDtypeStruct((t, D_INNER), BF16)] * 2,
        scratch_shapes=[pltpu.VMEM((SSD_STATE, D_INNER), F32)] * 2,
        compiler_params=_cparams(("parallel", "arbitrary")),
        name="ssd",
    )(xbc, xbc, xbc, dt_raw, xbc, xbc, xbc, dt_raw, dtb, alog)


QKN_TM = 512


def _qknorm_kernel(q_ref, k_ref, gq_ref, gk_ref, qo_ref, ko_ref):
    r = lax.broadcasted_iota(jnp.int32, (LANES, LANES), 0) // NA_HEAD_DIM
    c = lax.broadcasted_iota(jnp.int32, (LANES, LANES), 1) // NA_HEAD_DIM
    same_head = jnp.where(r == c, 1.0, 0.0).astype(BF16)
    for src, g_ref, dst, scale in ((q_ref, gq_ref, qo_ref, NA_HEAD_DIM ** -0.5), (k_ref, gk_ref, ko_ref, None)):
        for p in range(NA_WIDTH // LANES):
            sl = slice(p * LANES, (p + 1) * LANES)
            x = src[:, sl].astype(F32)
            sq = x * x
            hi = sq.astype(BF16)
            lo = (sq - hi.astype(F32)).astype(BF16)
            ss = (jnp.dot(hi, same_head, preferred_element_type=F32)
                  + jnp.dot(lo, same_head, preferred_element_type=F32))
            y = x * lax.rsqrt(ss / NA_HEAD_DIM + EPS) * g_ref[...]
            if scale is not None:
                y = y * scale
            dst[:, sl] = y.astype(BF16)


def _qknorm(proj, q_norm_g, k_norm_g):
    t = proj.shape[0]
    tm = QKN_TM
    reps = LANES // NA_HEAD_DIM
    gq = jnp.tile(q_norm_g, reps).reshape(1, LANES)
    gk = jnp.tile(k_norm_g, reps).reshape(1, LANES)
    qcol = COL_QKV // NA_WIDTH
    return pl.pallas_call(
        _qknorm_kernel,
        grid=(t // tm,),
        in_specs=[
            pl.BlockSpec((tm, NA_WIDTH), lambda i: (i, qcol)),
            pl.BlockSpec((tm, NA_WIDTH), lambda i: (i, qcol + 1)),
            pl.BlockSpec((1, LANES), lambda i: (0, 0)),
            pl.BlockSpec((1, LANES), lambda i: (0, 0)),
        ],
        out_specs=[pl.BlockSpec((tm, NA_WIDTH), lambda i: (i, 0))] * 2,
        out_shape=[jax.ShapeDtypeStruct((t, NA_WIDTH), BF16)] * 2,
        compiler_params=_cparams(("parallel",)),
        name="qknorm",
    )(proj, proj, gq, gk)


def _natten_kernel(*refs):
    q_ref = refs[0]
    k_refs = refs[1:1 + WIN_ROWS]
    v_refs = refs[1 + WIN_ROWS:1 + 2 * WIN_ROWS]
    bias_ref = refs[1 + 2 * WIN_ROWS]
    o_ref = refs[2 + 2 * WIN_ROWS]
    first = lax.broadcasted_iota(jnp.int32, (GRID_W, LANES), 1) < NA_HEAD_DIM
    for p in range(NA_WIDTH // LANES):
        sl = slice(p * LANES, (p + 1) * LANES)
        qp = q_ref[:, sl].astype(F32)
        qrows = jnp.concatenate([jnp.where(first, qp, 0.0), jnp.where(first, 0.0, qp)], axis=0).astype(BF16)
        kp = jnp.concatenate([r[:, sl] for r in k_refs], axis=0)
        vp = jnp.concatenate([r[:, sl] for r in v_refs], axis=0)
        st = lax.dot_general(kp, qrows, (((1,), (1,)), ((), ())), preferred_element_type=F32)
        st = st + bias_ref[0, p]
        m = jnp.max(st, axis=0, keepdims=True)
        pe = jnp.exp(st - m)
        inv = 1.0 / jnp.sum(pe, axis=0, keepdims=True)
        pn = (pe * inv).astype(BF16)
        o2 = lax.dot_general(pn, vp, (((0,), (0,)), ((), ())), preferred_element_type=F32)
        o_ref[:, sl] = jnp.where(first, o2[:GRID_W], o2[GRID_W:]).astype(BF16)


def _na_bias_table(rpb):
    cols = jnp.arange(GRID_W)
    c0 = jnp.clip(cols - WIN_COLS // 2, 0, GRID_W - WIN_COLS)
    j = jnp.arange(GRID_W)
    valid = (j[None, :] >= c0[:, None]) & (j[None, :] < c0[:, None] + WIN_COLS)
    k = jnp.arange(2 * WIN_COLS - 1)
    onehot = (k[:, None, None] == j[None, None, :] - cols[None, :, None] + (WIN_COLS - 1)).astype(F32)
    toep = jnp.einsum("hrk,kcj->hrcj", rpb.astype(F32), onehot, precision=lax.Precision.HIGHEST)
    toep = jnp.where(valid[None, None], toep, -jnp.inf)
    per_delta = jnp.stack([toep[:, WIN_ROWS - 1 - d:2 * WIN_ROWS - 1 - d] for d in range(WIN_ROWS)], axis=0)
    hp = LANES // NA_HEAD_DIM
    t6 = per_delta.reshape(WIN_ROWS, NA_HEADS // hp, hp, WIN_ROWS, GRID_W, GRID_W)
    t6 = jnp.transpose(t6, (0, 1, 3, 5, 2, 4))
    return t6.reshape(WIN_ROWS, NA_HEADS // hp, WIN_ROWS * GRID_W, hp * GRID_W)


def _natten(qn, kn, proj, rpb, nseq, seqlen):
    t = qn.shape[0]
    rows = seqlen // GRID_W
    assert rows >= WIN_ROWS
    r0 = lambda r: jnp.clip(r - WIN_ROWS // 2, 0, rows - WIN_ROWS)
    vcol = COL_QKV // NA_WIDTH + 2
    tbl = _na_bias_table(rpb)

    def win_spec(i, colblk):
        return pl.BlockSpec((GRID_W, NA_WIDTH), lambda s, r: (s * rows + r0(r) + i, colblk))

    return pl.pallas_call(
        _natten_kernel,
        grid=(nseq, rows),
        in_specs=([pl.BlockSpec((GRID_W, NA_WIDTH), lambda s, r: (s * rows + r, 0))]
                  + [win_spec(i, 0) for i in range(WIN_ROWS)]
                  + [win_spec(i, vcol) for i in range(WIN_ROWS)]
                  + [pl.BlockSpec((1,) + tbl.shape[1:], lambda s, r: (r - r0(r), 0, 0, 0))]),
        out_specs=pl.BlockSpec((GRID_W, NA_WIDTH), lambda s, r: (s * rows + r, 0)),
        out_shape=jax.ShapeDtypeStruct((t, NA_WIDTH), BF16),
        compiler_params=_cparams(("parallel", "arbitrary")),
        name="natten",
    )(qn, *([kn] * WIN_ROWS), *([proj] * WIN_ROWS), tbl)


OUTPROJ_TM = 512


def _outproj_kernel(na, yf_ref, yb_ref, xs_ref, z_ref, g1_ref, g2_ref, yna_ref, xa_ref, xb_ref, w1_ref, w2_ref,
                    dsk_ref, ng_ref, gffn_ref, x1_ref, t_ref):
    y = yf_ref[...].astype(F32) + yb_ref[...].astype(F32) + dsk_ref[...] * xs_ref[...].astype(F32)
    y = y * jax.nn.silu(z_ref[...].astype(F32))
    yn = _rms(y, ng_ref[...]).astype(BF16)
    br_ssd = jnp.dot(yn, w1_ref[...], preferred_element_type=F32)
    br_na = jnp.dot(yna_ref[...], w2_ref[...], preferred_element_type=F32)
    mix = (jax.nn.sigmoid(g1_ref[...].astype(F32)) * br_ssd
           + jax.nn.sigmoid(g2_ref[...].astype(F32)) * br_na)
    x1 = _cat_tile(na, xa_ref, xb_ref) + mix
    x1_ref[...] = x1
    t_ref[...] = pltpu.bitcast(_rms(x1, gffn_ref[...]).astype(BF16), jnp.uint32)


def _outproj(yf, yb, xbc, proj, yna, xa, xb, w_out, d_skip, ssd_norm_g, g_ffn):
    t = xa.shape[0] + xb.shape[0]
    tm = OUTPROJ_TM
    na = xa.shape[0] // tm
    tok = lambda w, cb: pl.BlockSpec((tm, w), lambda i: (i, cb))
    row = lambda w: pl.BlockSpec((1, w), lambda i: (0, 0))
    dsk = jnp.repeat(d_skip, SSD_HEAD_DIM).reshape(1, D_INNER)
    return pl.pallas_call(
        functools.partial(_outproj_kernel, na),
        grid=(t // tm,),
        in_specs=[
            tok(D_INNER, 0), tok(D_INNER, 0), tok(D_INNER, 0),
            tok(D_INNER, COL_Z // D_INNER),
            tok(D_MODEL, COL_GATE // D_MODEL), tok(D_MODEL, COL_GATE // D_MODEL + 1),
            tok(NA_WIDTH, 0), *_cat_specs(tm, D_MODEL, na, 1),
            pl.BlockSpec((D_INNER, D_MODEL), lambda i: (0, 0)),
            pl.BlockSpec((NA_WIDTH, D_MODEL), lambda i: (D_INNER // NA_WIDTH, 0)),
            row(D_INNER), row(D_INNER), row(D_MODEL),
        ],
        out_specs=[tok(D_MODEL, 0), pl.BlockSpec((tm // 2, D_MODEL), lambda i: (i, 0))],
        out_shape=[jax.ShapeDtypeStruct((t, D_MODEL), F32), jax.ShapeDtypeStruct((t // 2, D_MODEL), jnp.uint32)],
        compiler_params=_cparams(("parallel",)),
        name="outproj",
    )(yf, yb, xbc, proj, proj, proj, yna, xa, xb, w_out, w_out, dsk,
      ssd_norm_g.reshape(1, D_INNER), g_ffn.reshape(1, D_MODEL))


ROUTE_TR = 256
NEG_INF = float("-inf")
RANK_NONE = 64.0


def _top_values(ss, k, out_refs=None, want_rank=None):
    n = len(ss)
    ss = list(ss)
    out_refs = out_refs or [None] * n
    want_rank = want_rank or [False] * n
    ranks = [jnp.full(s.shape, RANK_NONE, F32) if w else None for s, w in zip(ss, want_rank)]
    mx = [None] * n
    for r in range(k):
        for i in range(n):
            mx[i] = jnp.max(ss[i], axis=0, keepdims=True)
            if out_refs[i] is not None:
                out_refs[i][r:r + 1, :] = mx[i]
            hit = ss[i] == mx[i]
            if ranks[i] is not None:
                ranks[i] = jnp.where(hit, float(r), ranks[i])
            ss[i] = jnp.where(hit, NEG_INF, ss[i])
    return mx, ranks


ROUTE_HEAD_GROUP = 4


def _route_kernel(t_ref, wpq_ref, sk_ref, rank2_ref, b_ref, cnt_ref, a_ref, top1, top2):
    half = PEER_QUERY_DIM // 2
    sub = 8
    tokens = pltpu.bitcast(t_ref[...], BF16)
    row = lax.broadcasted_iota(jnp.int32, (sub, tokens.shape[0]), 0)
    q = jnp.dot(tokens, wpq_ref[...], preferred_element_type=F32).astype(BF16)
    for h0 in range(0, PEER_HEADS, ROUTE_HEAD_GROUP):
        heads = range(h0, h0 + ROUTE_HEAD_GROUP)
        s1, s2 = [], []
        for h in heads:
            for side, dst in ((0, s1), (1, s2)):
                qb = q[:, (2 * h + side) * half:(2 * h + side + 1) * half]
                dst.append(lax.dot_general(sk_ref[side], qb, (((1,), (1,)), ((), ())),
                                           preferred_element_type=F32))
        _, ranks = _top_values(s1 + s2, PEER_TOPK, [top1.at[h] for h in heads] + [top2.at[h] for h in heads],
                               [False] * len(s1) + [True] * len(s2))
        rank2 = ranks[len(s1):]
        cands = []
        for h in heads:
            t1, t2 = top1.at[h], top2.at[h]
            tiles = [t1[0:1, :] + t2[0:sub, :], t1[0:1, :] + t2[sub:, :]]
            for r1 in range(1, sub):
                tile = t1[r1:r1 + 1, :] + t2[0:sub, :]
                tiles.append(jnp.where(row < PEER_TOPK // (r1 + 1), tile, NEG_INF))
            tiles.append(t1[sub:, :] + t2[0:1, :])
            cands.append(jnp.concatenate(tiles, axis=0))
        taus, _ = _top_values(cands, PEER_TOPK)
        for g, h in enumerate(heads):
            t1, t2 = top1.at[h], top2.at[h]
            cand, tau = cands[g], taus[g]
            m1 = t1[0:1, :]
            m2 = t2[0:1, :]
            z = jnp.sum(jnp.where(cand >= tau, jnp.exp(cand - (m1 + m2)), 0.0), axis=0, keepdims=True)
            cnt = jnp.zeros_like(s1[g])
            for r2 in range(PEER_TOPK):
                cnt = cnt + jnp.where((s1[g] + t2[r2:r2 + 1, :]) >= tau, 1.0, 0.0)
            rank2_ref[h] = pltpu.bitcast(rank2[g].astype(BF16), jnp.uint32)
            b_ref[h] = pltpu.bitcast((jnp.exp(s2[g] - m2) / z).astype(BF16), jnp.uint32)
            cnt_ref[h] = cnt
            a_ref[h] = jnp.exp(s1[g] - m1)


def _route(tn, w_pq, sub_keys):
    t = 2 * tn.shape[0]
    tr = ROUTE_TR
    qd = PEER_HEADS * PEER_QUERY_DIM
    out_spec = pl.BlockSpec((PEER_HEADS, PEER_N_KEYS, tr), lambda i: (0, 0, i))
    key_shape = (PEER_HEADS, PEER_N_KEYS, t)
    pair_spec = pl.BlockSpec((PEER_HEADS, PEER_N_KEYS // 2, tr), lambda i: (0, 0, i))
    pair_shape = (PEER_HEADS, PEER_N_KEYS // 2, t)
    return pl.pallas_call(
        _route_kernel,
        grid=(t // tr,),
        in_specs=[
            pl.BlockSpec((tr // 2, D_MODEL), lambda i: (i, 0)),
            pl.BlockSpec((D_MODEL, qd), lambda i: (0, 0)),
            pl.BlockSpec((2, PEER_N_KEYS, PEER_QUERY_DIM // 2), lambda i: (0, 0, 0)),
        ],
        out_specs=[pair_spec, pair_spec, out_spec, out_spec],
        out_shape=[jax.ShapeDtypeStruct(pair_shape, jnp.uint32), jax.ShapeDtypeStruct(pair_shape, jnp.uint32),
                   jax.ShapeDtypeStruct(key_shape, F32), jax.ShapeDtypeStruct(key_shape, F32)],
        scratch_shapes=[pltpu.VMEM((PEER_HEADS, PEER_TOPK, tr), F32)] * 2,
        compiler_params=_cparams(("parallel",)),
        name="route",
    )(tn, w_pq, sub_keys)


PEER_TP = 512
PEER_I1 = 8
PEER_EB = PEER_I1 * PEER_N_KEYS
NT_DIMS = (((1,), (1,)), ((), ()))
GATE_ROWS = PEER_N_KEYS


def _gate_block(first_keys, st_slot, g_slot, rank2_ref, b_ref, cnt_ref, a_ref):
    rows = GATE_ROWS
    for i in first_keys:
        for part in range(PEER_N_KEYS // rows):
            r0 = part * rows
            e0 = i * PEER_N_KEYS + r0
            for ck in range(st_slot.shape[1] // LANES):
                cols = slice(ck * LANES, (ck + 1) * LANES)
                wgt = jnp.zeros((rows, LANES), BF16)
                for h in range(PEER_HEADS):
                    cnt = cnt_ref[h, i:i + 1, cols].astype(BF16)
                    a1 = a_ref[h, i:i + 1, cols].astype(BF16)
                    rank2 = pltpu.bitcast(rank2_ref[h, r0 // 2:(r0 + rows) // 2, cols], BF16)
                    b2 = pltpu.bitcast(b_ref[h, r0 // 2:(r0 + rows) // 2, cols], BF16)
                    wgt = wgt + jnp.where(rank2 < cnt, b2, jnp.zeros_like(b2)) * a1
                pre = st_slot[e0:e0 + rows, cols].astype(BF16)
                act = 0.5 * pre * (1.0 + lax.erf(pre * (2.0 ** -0.5)))
                g_slot[e0 // 2:(e0 + rows) // 2, cols] = pltpu.bitcast(act * wgt, jnp.uint32)


def _peer_kernel(na, t_ref, u_ref, vt_ref, rank2_ref, b_ref, cnt_lo, a_lo, cnt_hi, a_hi, x1_ref, oa_ref, ob_ref,
                 acc, st0, st1, gated0, gated1):
    jj = pl.program_id(1)
    eb = PEER_EB
    st = (st0, st1)
    gated = (gated0, gated1)

    @pl.when(jj == 0)
    def _():
        for ref in (acc, st0, st1, gated0, gated1):
            ref[...] = jnp.zeros_like(ref)

    for k in range(2):
        other = 1 - k
        cnt_ref, a_ref = (cnt_lo, a_lo) if k == 0 else (cnt_hi, a_hi)
        acc[...] = jnp.dot(pltpu.bitcast(vt_ref[:, k * eb:(k + 1) * eb], BF16),
                           pltpu.bitcast(gated[k][...], BF16), preferred_element_type=F32) + acc[...]
        _gate_block(range(PEER_I1), st[other], gated[other], rank2_ref, b_ref, cnt_ref, a_ref)
        st[k][...] = lax.dot_general(pltpu.bitcast(u_ref[k * eb // 2:(k + 1) * eb // 2, :], BF16),
                                     pltpu.bitcast(t_ref[...], BF16), NT_DIMS,
                                     preferred_element_type=F32)

    last = jj == pl.num_programs(1) - 1
    for o_ref, mine in ((oa_ref, pl.program_id(0) < na), (ob_ref, pl.program_id(0) >= na)):
        @pl.when(last & mine)
        def _(o_ref=o_ref):
            o_ref[...] = x1_ref[...] + acc[...].T


def _peer(tn, u_bf, vt_bf, rank2, b, cnt, a, x1, ta):
    t = x1.shape[0]
    tp, eb = PEER_TP, PEER_EB
    assert ta % tp == 0
    na = ta // tp
    nb = 2 * u_bf.shape[0] // eb
    assert nb % 2 == 0
    pair_spec = pl.BlockSpec((PEER_HEADS, PEER_N_KEYS // 2, tp), lambda i, j: (0, 0, i))
    lo_spec = pl.BlockSpec((PEER_HEADS, PEER_I1, tp), lambda i, j: (0, jnp.maximum(2 * j - 1, 0), i))
    hi_spec = pl.BlockSpec((PEER_HEADS, PEER_I1, tp), lambda i, j: (0, jnp.minimum(2 * j, nb - 1), i))
    return pl.pallas_call(
        functools.partial(_peer_kernel, na),
        grid=(t // tp, nb // 2 + 1),
        in_specs=[
            pl.BlockSpec((tp // 2, D_MODEL), lambda i, j: (i, 0)),
            pl.BlockSpec((eb, D_MODEL), lambda i, j: (jnp.minimum(j, nb // 2 - 1), 0)),
            pl.BlockSpec((D_MODEL // 2, 2 * eb), lambda i, j: (0, jnp.maximum(j - 1, 0))),
            pair_spec, pair_spec, lo_spec, lo_spec, hi_spec, hi_spec,
            pl.BlockSpec((tp, D_MODEL), lambda i, j: (i, 0)),
        ],
        out_specs=_cat_specs(tp, D_MODEL, na, 2),
        out_shape=[jax.ShapeDtypeStruct((ta, D_MODEL), F32), jax.ShapeDtypeStruct((t - ta, D_MODEL), F32)],
        scratch_shapes=[pltpu.VMEM((D_MODEL, tp), F32), pltpu.VMEM((eb, tp), F32), pltpu.VMEM((eb, tp), F32),
                        pltpu.VMEM((eb // 2, tp), jnp.uint32), pltpu.VMEM((eb // 2, tp), jnp.uint32)],
        compiler_params=_cparams(("arbitrary", "arbitrary")),
        name="peer",
    )(tn, u_bf, vt_bf, rank2, b, cnt, a, cnt, a, x1)


def _pack_rows(x):
    n2, c = x.shape
    pairs = lax.bitcast_convert_type(x, jnp.uint16).astype(jnp.uint32).reshape(n2 // 2, 2 * c)
    return pairs[:, :c] | (pairs[:, c:] << 16)


def _regroup_w_in(w_in):
    pad = jnp.zeros((D_MODEL, PROJ_COLS - COL_DT - 2 * SSD_HEADS), w_in.dtype)
    return jnp.concatenate([w_in[:, OFF_XBC:OFF_DT], w_in[:, OFF_QKV:OFF_GATE], w_in[:, :OFF_XBC],
                            w_in[:, OFF_GATE:], w_in[:, OFF_DT:OFF_QKV], pad], axis=1).astype(BF16)


def _encoder_layer(xa, xb, seqlen, g_mix, w_in, conv_w, conv_b, dt_bias, a_log, d_skip, ssd_norm_g,
                   q_norm_g, k_norm_g, rpb, w_out, g_ffn, w_pq, sub_keys, expert_u, expert_v):
    nseq = (xa.shape[0] + xb.shape[0]) // seqlen
    proj, dt_raw = _inproj(xa, xb, g_mix, _regroup_w_in(w_in))
    xbc = _conv(proj, conv_w, conv_b, nseq, seqlen)
    yf, yb = _ssd(xbc, dt_raw, dt_bias, a_log, nseq, seqlen)
    qn, kn = _qknorm(proj, q_norm_g, k_norm_g)
    yna = _natten(qn, kn, proj, rpb, nseq, seqlen)
    x1, tn = _outproj(yf, yb, xbc, proj, yna, xa, xb, w_out.astype(BF16), d_skip, ssd_norm_g, g_ffn)
    rank2, b, cnt, a = _route(tn, w_pq.astype(BF16), sub_keys.astype(BF16))
    return _peer(tn, _pack_rows(expert_u.astype(BF16)), _pack_rows(expert_v.T.astype(BF16)), rank2, b, cnt, a, x1,
                 xa.shape[0])


def kernel(x_prompt, x_sample, g_mix, w_in, conv_w, conv_b, dt_bias, a_log, d_skip, ssd_norm_g, q_norm_g,
           k_norm_g, rpb, w_out, g_ffn, w_pq, sub_keys, expert_u, expert_v):
    assert x_prompt.shape[1:] == x_sample.shape[1:]
    seqlen = x_prompt.shape[1]
    xa = x_prompt.reshape(-1, D_MODEL)
    xb = x_sample.reshape(-1, D_MODEL)
    for i in range(g_mix.shape[0]):
        xa, xb = _encoder_layer(xa, xb, seqlen, g_mix[i], w_in[i], conv_w[i], conv_b[i], dt_bias[i], a_log[i],
                                d_skip[i], ssd_norm_g[i], q_norm_g[i], k_norm_g[i], rpb[i], w_out[i], g_ffn[i],
                                w_pq[i], sub_keys[i], expert_u[i], expert_v[i])
    return (xa.reshape(x_prompt.shape), xb.reshape(x_sample.shape))
```

```python
import functools

import jax
import jax.numpy as jnp
from jax import lax
from jax.experimental import pallas as pl
from jax.experimental.pallas import tpu as pltpu

F32 = jnp.float32
BF16 = jnp.bfloat16

D_MODEL = 1024
GRID_W = 64
EPS = 1e-6
D_INNER = 2048
SSD_HEAD_DIM = 64
SSD_HEADS = 32
SSD_GROUPS = 4
SSD_STATE = 128
CONV_W = 5
CHUNK = 128
CONV_DIM = D_INNER + 2 * SSD_GROUPS * SSD_STATE
NA_HEADS = 16
NA_HEAD_DIM = 64
NA_WIDTH = 1024
WIN_ROWS = 8
WIN_COLS = 16
PEER_HEADS = 8
PEER_N_KEYS = 128
PEER_TOPK = 16
PEER_QUERY_DIM = 256
OFF_XBC = D_INNER
OFF_DT = OFF_XBC + CONV_DIM
OFF_QKV = OFF_DT + 2 * SSD_HEADS
OFF_GATE = OFF_QKV + 3 * NA_WIDTH

LANES = 128
MXU_COLS = 256
COL_QKV = CONV_DIM
COL_Z = COL_QKV + 3 * NA_WIDTH
COL_GATE = COL_Z + D_INNER
COL_DT = COL_GATE + 2 * D_MODEL
INPROJ_TN = 6 * MXU_COLS
PROJ_COLS = -(-(COL_DT + LANES) // INPROJ_TN) * INPROJ_TN
DT_LOCAL = COL_DT - (PROJ_COLS - INPROJ_TN)

VMEM_LIMIT = 56 * 1024 * 1024


def _cparams(sem, flags=None):
    return pltpu.CompilerParams(dimension_semantics=sem, vmem_limit_bytes=VMEM_LIMIT, flags=flags)


def _rms(x, g):
    return x * lax.rsqrt(jnp.mean(x * x, axis=-1, keepdims=True) + EPS) * g


INPROJ_TM = 1024


def _cat_specs(tm, width, na, grid_rank):
    if grid_rank == 1:
        return [pl.BlockSpec((tm, width), lambda i: (jnp.minimum(i, na - 1), 0)),
                pl.BlockSpec((tm, width), lambda i: (jnp.maximum(i - na, 0), 0))]
    return [pl.BlockSpec((tm, width), lambda i, j: (jnp.minimum(i, na - 1), 0)),
            pl.BlockSpec((tm, width), lambda i, j: (jnp.maximum(i - na, 0), 0))]


def _cat_tile(na, a_ref, b_ref):
    return jnp.where(pl.program_id(0) < na, a_ref[...], b_ref[...])


def _inproj_kernel(na, xa_ref, xb_ref, g_ref, w_ref, o_ref, dt_ref, h_scr):
    j = pl.program_id(1)

    @pl.when(j == 0)
    def _():
        h_scr[...] = _rms(_cat_tile(na, xa_ref, xb_ref), g_ref[...]).astype(BF16)

    acc = jnp.dot(h_scr[...], w_ref[...], preferred_element_type=F32)
    o_ref[...] = acc.astype(BF16)

    @pl.when(j == pl.num_programs(1) - 1)
    def _():
        dt_ref[...] = acc[:, DT_LOCAL:DT_LOCAL + LANES]


def _inproj(xa, xb, g_mix, w_cat):
    t = xa.shape[0] + xb.shape[0]
    tm, tn = INPROJ_TM, INPROJ_TN
    assert xa.shape[0] % tm == 0 and xb.shape[0] % tm == 0 and PROJ_COLS % tn == 0 and COL_DT >= PROJ_COLS - tn
    na = xa.shape[0] // tm
    return pl.pallas_call(
        functools.partial(_inproj_kernel, na),
        grid=(t // tm, PROJ_COLS // tn),
        in_specs=_cat_specs(tm, D_MODEL, na, 2) + [
            pl.BlockSpec((1, D_MODEL), lambda i, j: (0, 0)),
            pl.BlockSpec((D_MODEL, tn), lambda i, j: (0, j)),
        ],
        out_specs=[
            pl.BlockSpec((tm, tn), lambda i, j: (i, j)),
            pl.BlockSpec((tm, LANES), lambda i, j: (i, 0)),
        ],
        out_shape=[
            jax.ShapeDtypeStruct((t, PROJ_COLS), BF16),
            jax.ShapeDtypeStruct((t, LANES), F32),
        ],
        scratch_shapes=[pltpu.VMEM((tm, D_MODEL), BF16)],
        compiler_params=_cparams(("parallel", "arbitrary")),
        name="inproj",
    )(xa, xb, g_mix.reshape(1, D_MODEL), w_cat)


CONV_TL = 512
CONV_CB = 1024
HALO = 16
PAD = 8


def _conv_kernel(cur_ref, prev_ref, next_ref, w_ref, b_ref, o_ref, ext):
    i = pl.program_id(1)
    tl = cur_ref.shape[0]
    prev = prev_ref[...].astype(F32)[HALO - PAD:, :]
    nxt = next_ref[...].astype(F32)[:PAD, :]
    ext[0:PAD, :] = jnp.where(i > 0, prev, 0.0)
    ext[PAD:PAD + tl, :] = cur_ref[...].astype(F32)
    ext[PAD + tl:, :] = jnp.where(i < pl.num_programs(1) - 1, nxt, 0.0)
    acc = jnp.broadcast_to(b_ref[...], o_ref.shape)
    for k in range(CONV_W):
        acc = acc + w_ref[k:k + 1, :] * ext[pl.ds(PAD - CONV_W // 2 + k, tl), :]
    o_ref[...] = (acc * jax.nn.sigmoid(acc)).astype(BF16)


def _conv(proj, conv_w, conv_b, nseq, seqlen):
    t = proj.shape[0]
    tl, cb = CONV_TL, CONV_CB
    nl = seqlen // tl
    assert seqlen % tl == 0 and CONV_DIM % cb == 0
    return pl.pallas_call(
        _conv_kernel,
        grid=(nseq, nl, CONV_DIM // cb),
        in_specs=[
            pl.BlockSpec((tl, cb), lambda s, i, c: (s * nl + i, c)),
            pl.BlockSpec((HALO, cb), lambda s, i, c: (jnp.maximum((s * nl + i) * (tl // HALO) - 1, 0), c)),
            pl.BlockSpec((HALO, cb),
                         lambda s, i, c: (jnp.minimum((s * nl + i + 1) * (tl // HALO), t // HALO - 1), c)),
            pl.BlockSpec((CONV_W, cb), lambda s, i, c: (0, c)),
            pl.BlockSpec((1, cb), lambda s, i, c: (0, c)),
        ],
        out_specs=pl.BlockSpec((tl, cb), lambda s, i, c: (s * nl + i, c)),
        out_shape=jax.ShapeDtypeStruct((t, CONV_DIM), BF16),
        scratch_shapes=[pltpu.VMEM((tl + 2 * PAD, cb), F32)],
        compiler_params=_cparams(("parallel", "parallel", "parallel")),
        name="conv",
    )(proj, proj, proj, conv_w, conv_b.reshape(1, CONV_DIM))


def _cumsum_rows(x):
    row = lax.broadcasted_iota(jnp.int32, x.shape, 0)
    d = 1
    while d < x.shape[0]:
        x = x + jnp.where(row >= d, pltpu.roll(x, d, 0), 0.0)
        d *= 2
    return x


def _ssd_direction(reverse, xs_ref, b_ref, c_ref, dt_ref, dtb_ref, alog_ref, y_ref, st):
    row = lax.broadcasted_iota(jnp.int32, (CHUNK, CHUNK), 0)
    col = lax.broadcasted_iota(jnp.int32, (CHUNK, CHUNK), 1)
    lane1 = lax.broadcasted_iota(jnp.int32, (1, LANES), 1)
    dt_pre = dt_ref[...] + dtb_ref[...]
    dt = jnp.maximum(dt_pre, 0.0) + jnp.log1p(jnp.exp(-jnp.abs(dt_pre)))
    dta = dt * (-jnp.exp(alog_ref[...]))
    cs = _cumsum_rows(dta)
    total = cs[CHUNK - 1:CHUNK, :]
    if reverse:
        cum = total - cs + dta
        wexp = cs - dta
        tri = col >= row
    else:
        cum = cs
        wexp = total - cs
        tri = row >= col
    w = jnp.exp(wexp) * dt
    cum_t = cum.T
    dt_t = dt.T
    w_t = w.T
    dec = jnp.exp(total)
    base = SSD_HEADS if reverse else 0
    pairs_per_group = SSD_HEADS // SSD_GROUPS // 2
    for g in range(SSD_GROUPS):
        bg = b_ref[:, g * SSD_STATE:(g + 1) * SSD_STATE]
        cg = c_ref[:, g * SSD_STATE:(g + 1) * SSD_STATE]
        gmat = lax.dot_general(cg, bg, (((1,), (1,)), ((), ())), preferred_element_type=F32)
        bg_t = bg.astype(F32).T
        cg32 = cg.astype(F32)
        for hp in range(pairs_per_group):
            j = g * pairs_per_group + hp
            sl = slice(j * LANES, (j + 1) * LANES)
            xs_pair = xs_ref[:, sl].astype(F32)
            st_pair = st[:, sl]
            y_acc = jnp.zeros((CHUNK, LANES), F32)
            s_acc = jnp.zeros((SSD_STATE, LANES), F32)
            dec_pair = jnp.zeros((1, LANES), F32)
            for e in range(2):
                ln = base + 2 * j + e
                in_head = (col // SSD_HEAD_DIM) == e
                colb = jnp.broadcast_to(cum[:, ln:ln + 1], (CHUNK, CHUNK))
                decay = jnp.where(tri, jnp.exp(colb - cum_t[ln:ln + 1, :]), 0.0)
                m = gmat * decay * dt_t[ln:ln + 1, :]
                ce = cg32 * jnp.exp(colb)
                lhs = jnp.concatenate([m, ce], axis=1).astype(BF16)
                xs_m = jnp.where(in_head, xs_pair, 0.0).astype(BF16)
                st_m = jnp.where(in_head, st_pair, 0.0).astype(BF16)
                rhs = jnp.concatenate([xs_m, st_m], axis=0)
                y_acc = y_acc + jnp.dot(lhs, rhs, preferred_element_type=F32)
                btw = (bg_t * w_t[ln:ln + 1, :]).astype(BF16)
                s_acc = s_acc + jnp.dot(btw, xs_m, preferred_element_type=F32)
                dec_pair = jnp.where((lane1 // SSD_HEAD_DIM) == e,
                                     jnp.broadcast_to(dec[:, ln:ln + 1], (1, LANES)), dec_pair)
            y_ref[:, sl] = y_acc.astype(BF16)
            st[:, sl] = st_pair * dec_pair + s_acc


def _ssd_kernel(xs_f, b_f, c_f, dt_f, xs_b, b_b, c_b, dt_b, dtb_ref, alog_ref, yf_ref, yb_ref, st_f, st_b):
    @pl.when(pl.program_id(1) == 0)
    def _():
        st_f[...] = jnp.zeros_like(st_f)
        st_b[...] = jnp.zeros_like(st_b)

    _ssd_direction(False, xs_f, b_f, c_f, dt_f, dtb_ref, alog_ref, yf_ref, st_f)
    _ssd_direction(True, xs_b, b_b, c_b, dt_b, dtb_ref, alog_ref, yb_ref, st_b)


def _ssd(xbc, dt_raw, dt_bias, a_log, nseq, seqlen):
    t = xbc.shape[0]
    nc = seqlen // CHUNK
    bn = SSD_GROUPS * SSD_STATE
    fwd = lambda s, c: s * nc + c
    bwd = lambda s, c: s * nc + (nc - 1 - c)

    def specs(rb):
        return [
            pl.BlockSpec((CHUNK, D_INNER), lambda s, c: (rb(s, c), 0)),
            pl.BlockSpec((CHUNK, bn), lambda s, c: (rb(s, c), D_INNER // bn)),
            pl.BlockSpec((CHUNK, bn), lambda s, c: (rb(s, c), D_INNER // bn + 1)),
        ]

    dt_spec = lambda rb: pl.BlockSpec((CHUNK, LANES), lambda s, c: (rb(s, c), 0))
    row_spec = pl.BlockSpec((1, LANES), lambda s, c: (0, 0))
    pad = jnp.zeros((LANES - 2 * SSD_HEADS,), F32)
    dtb = jnp.concatenate([dt_bias.reshape(-1), pad]).reshape(1, LANES)
    alog = jnp.concatenate([a_log.reshape(-1), pad]).reshape(1, LANES)
    return pl.pallas_call(
        _ssd_kernel,
        grid=(nseq, nc),
        in_specs=specs(fwd) + [dt_spec(fwd)] + specs(bwd) + [dt_spec(bwd)] + [row_spec, row_spec],
        out_specs=[
            pl.BlockSpec((CHUNK, D_INNER), lambda s, c: (fwd(s, c), 0)),
            pl.BlockSpec((CHUNK, D_INNER), lambda s, c: (bwd(s, c), 0)),
        ],
        out_shape=[jax.ShapeDtypeStruct((t, D_INNER), BF16)] * 2,
        scratch_shapes=[pltpu.VMEM((SSD_STATE, D_INNER), F32)] * 2,
        compiler_params=_cparams(("parallel", "arbitrary")),
        name="ssd",
    )(xbc, xbc, xbc, dt_raw, xbc, xbc, xbc, dt_raw, dtb, alog)


QKN_TM = 512


def _qknorm_kernel(q_ref, k_ref, gq_ref, gk_ref, qo_ref, ko_ref):
    r = lax.broadcasted_iota(jnp.int32, (LANES, LANES), 0) // NA_HEAD_DIM
    c = lax.broadcasted_iota(jnp.int32, (LANES, LANES), 1) // NA_HEAD_DIM
    same_head = jnp.where(r == c, 1.0, 0.0).astype(BF16)
    for src, g_ref, dst, scale in ((q_ref, gq_ref, qo_ref, NA_HEAD_DIM ** -0.5), (k_ref, gk_ref, ko_ref, None)):
        for p in range(NA_WIDTH // LANES):
            sl = slice(p * LANES, (p + 1) * LANES)
            x = src[:, sl].astype(F32)
            sq = x * x
            hi = sq.astype(BF16)
            lo = (sq - hi.astype(F32)).astype(BF16)
            ss = (jnp.dot(hi, same_head, preferred_element_type=F32)
                  + jnp.dot(lo, same_head, preferred_element_type=F32))
            y = x * lax.rsqrt(ss / NA_HEAD_DIM + EPS) * g_ref[...]
            if scale is not None:
                y = y * scale
            dst[:, sl] = y.astype(BF16)


def _qknorm(proj, q_norm_g, k_norm_g):
    t = proj.shape[0]
    tm = QKN_TM
    reps = LANES // NA_HEAD_DIM
    gq = jnp.tile(q_norm_g, reps).reshape(1, LANES)
    gk = jnp.tile(k_norm_g, reps).reshape(1, LANES)
    qcol = COL_QKV // NA_WIDTH
    return pl.pallas_call(
        _qknorm_kernel,
        grid=(t // tm,),
        in_specs=[
            pl.BlockSpec((tm, NA_WIDTH), lambda i: (i, qcol)),
            pl.BlockSpec((tm, NA_WIDTH), lambda i: (i, qcol + 1)),
            pl.BlockSpec((1, LANES), lambda i: (0, 0)),
            pl.BlockSpec((1, LANES), lambda i: (0, 0)),
        ],
        out_specs=[pl.BlockSpec((tm, NA_WIDTH), lambda i: (i, 0))] * 2,
        out_shape=[jax.ShapeDtypeStruct((t, NA_WIDTH), BF16)] * 2,
        compiler_params=_cparams(("parallel",)),
        name="qknorm",
    )(proj, proj, gq, gk)


def _natten_kernel(*refs):
    q_ref = refs[0]
    k_refs = refs[1:1 + WIN_ROWS]
    v_refs = refs[1 + WIN_ROWS:1 + 2 * WIN_ROWS]
    bias_ref = refs[1 + 2 * WIN_ROWS]
    o_ref = refs[2 + 2 * WIN_ROWS]
    first = lax.broadcasted_iota(jnp.int32, (GRID_W, LANES), 1) < NA_HEAD_DIM
    for p in range(NA_WIDTH // LANES):
        sl = slice(p * LANES, (p + 1) * LANES)
        qp = q_ref[:, sl].astype(F32)
        qrows = jnp.concatenate([jnp.where(first, qp, 0.0), jnp.where(first, 0.0, qp)], axis=0).astype(BF16)
        kp = jnp.concatenate([r[:, sl] for r in k_refs], axis=0)
        vp = jnp.concatenate([r[:, sl] for r in v_refs], axis=0)
        st = lax.dot_general(kp, qrows, (((1,), (1,)), ((), ())), preferred_element_type=F32)
        st = st + bias_ref[0, p]
        m = jnp.max(st, axis=0, keepdims=True)
        pe = jnp.exp(st - m)
        inv = 1.0 / jnp.sum(pe, axis=0, keepdims=True)
        pn = (pe * inv).astype(BF16)
        o2 = lax.dot_general(pn, vp, (((0,), (0,)), ((), ())), preferred_element_type=F32)
        o_ref[:, sl] = jnp.where(first, o2[:GRID_W], o2[GRID_W:]).astype(BF16)


def _na_bias_table(rpb):
    cols = jnp.arange(GRID_W)
    c0 = jnp.clip(cols - WIN_COLS // 2, 0, GRID_W - WIN_COLS)
    j = jnp.arange(GRID_W)
    valid = (j[None, :] >= c0[:, None]) & (j[None, :] < c0[:, None] + WIN_COLS)
    k = jnp.arange(2 * WIN_COLS - 1)
    onehot = (k[:, None, None] == j[None, None, :] - cols[None, :, None] + (WIN_COLS - 1)).astype(F32)
    toep = jnp.einsum("hrk,kcj->hrcj", rpb.astype(F32), onehot, precision=lax.Precision.HIGHEST)
    toep = jnp.where(valid[None, None], toep, -jnp.inf)
    per_delta = jnp.stack([toep[:, WIN_ROWS - 1 - d:2 * WIN_ROWS - 1 - d] for d in range(WIN_ROWS)], axis=0)
    hp = LANES // NA_HEAD_DIM
    t6 = per_delta.reshape(WIN_ROWS, NA_HEADS // hp, hp, WIN_ROWS, GRID_W, GRID_W)
    t6 = jnp.transpose(t6, (0, 1, 3, 5, 2, 4))
    return t6.reshape(WIN_ROWS, NA_HEADS // hp, WIN_ROWS * GRID_W, hp * GRID_W)


def _natten(qn, kn, proj, rpb, nseq, seqlen):
    t = qn.shape[0]
    rows = seqlen // GRID_W
    assert rows >= WIN_ROWS
    r0 = lambda r: jnp.clip(r - WIN_ROWS // 2, 0, rows - WIN_ROWS)
    vcol = COL_QKV // NA_WIDTH + 2
    tbl = _na_bias_table(rpb)

    def win_spec(i, colblk):
        return pl.BlockSpec((GRID_W, NA_WIDTH), lambda s, r: (s * rows + r0(r) + i, colblk))

    return pl.pallas_call(
        _natten_kernel,
        grid=(nseq, rows),
        in_specs=([pl.BlockSpec((GRID_W, NA_WIDTH), lambda s, r: (s * rows + r, 0))]
                  + [win_spec(i, 0) for i in range(WIN_ROWS)]
                  + [win_spec(i, vcol) for i in range(WIN_ROWS)]
                  + [pl.BlockSpec((1,) + tbl.shape[1:], lambda s, r: (r - r0(r), 0, 0, 0))]),
        out_specs=pl.BlockSpec((GRID_W, NA_WIDTH), lambda s, r: (s * rows + r, 0)),
        out_shape=jax.ShapeDtypeStruct((t, NA_WIDTH), BF16),
        compiler_params=_cparams(("parallel", "arbitrary")),
        name="natten",
    )(qn, *([kn] * WIN_ROWS), *([proj] * WIN_ROWS), tbl)


OUTPROJ_TM = 512


def _outproj_kernel(na, yf_ref, yb_ref, xs_ref, z_ref, g1_ref, g2_ref, yna_ref, xa_ref, xb_ref, w1_ref, w2_ref,
                    dsk_ref, ng_ref, gffn_ref, x1_ref, t_ref):
    y = yf_ref[...].astype(F32) + yb_ref[...].astype(F32) + dsk_ref[...] * xs_ref[...].astype(F32)
    y = y * jax.nn.silu(z_ref[...].astype(F32))
    yn = _rms(y, ng_ref[...]).astype(BF16)
    br_ssd = jnp.dot(yn, w1_ref[...], preferred_element_type=F32)
    br_na = jnp.dot(yna_ref[...], w2_ref[...], preferred_element_type=F32)
    mix = (jax.nn.sigmoid(g1_ref[...].astype(F32)) * br_ssd
           + jax.nn.sigmoid(g2_ref[...].astype(F32)) * br_na)
    x1 = _cat_tile(na, xa_ref, xb_ref) + mix
    x1_ref[...] = x1
    t_ref[...] = pltpu.bitcast(_rms(x1, gffn_ref[...]).astype(BF16), jnp.uint32)


def _outproj(yf, yb, xbc, proj, yna, xa, xb, w_out, d_skip, ssd_norm_g, g_ffn):
    t = xa.shape[0] + xb.shape[0]
    tm = OUTPROJ_TM
    na = xa.shape[0] // tm
    tok = lambda w, cb: pl.BlockSpec((tm, w), lambda i: (i, cb))
    row = lambda w: pl.BlockSpec((1, w), lambda i: (0, 0))
    dsk = jnp.repeat(d_skip, SSD_HEAD_DIM).reshape(1, D_INNER)
    return pl.pallas_call(
        functools.partial(_outproj_kernel, na),
        grid=(t // tm,),
        in_specs=[
            tok(D_INNER, 0), tok(D_INNER, 0), tok(D_INNER, 0),
            tok(D_INNER, COL_Z // D_INNER),
            tok(D_MODEL, COL_GATE // D_MODEL), tok(D_MODEL, COL_GATE // D_MODEL + 1),
            tok(NA_WIDTH, 0), *_cat_specs(tm, D_MODEL, na, 1),
            pl.BlockSpec((D_INNER, D_MODEL), lambda i: (0, 0)),
            pl.BlockSpec((NA_WIDTH, D_MODEL), lambda i: (D_INNER // NA_WIDTH, 0)),
            row(D_INNER), row(D_INNER), row(D_MODEL),
        ],
        out_specs=[tok(D_MODEL, 0), pl.BlockSpec((tm // 2, D_MODEL), lambda i: (i, 0))],
        out_shape=[jax.ShapeDtypeStruct((t, D_MODEL), F32), jax.ShapeDtypeStruct((t // 2, D_MODEL), jnp.uint32)],
        compiler_params=_cparams(("parallel",)),
        name="outproj",
    )(yf, yb, xbc, proj, proj, proj, yna, xa, xb, w_out, w_out, dsk,
      ssd_norm_g.reshape(1, D_INNER), g_ffn.reshape(1, D_MODEL))


ROUTE_TR = 256
NEG_INF = float("-inf")
RANK_NONE = 64.0


def _top_values(ss, k, out_refs=None, want_rank=None):
    n = len(ss)
    ss = list(ss)
    out_refs = out_refs or [None] * n
    want_rank = want_rank or [False] * n
    ranks = [jnp.full(s.shape, RANK_NONE, F32) if w else None for s, w in zip(ss, want_rank)]
    mx = [None] * n
    for r in range(k):
        for i in range(n):
            mx[i] = jnp.max(ss[i], axis=0, keepdims=True)
            if out_refs[i] is not None:
                out_refs[i][r:r + 1, :] = mx[i]
            hit = ss[i] == mx[i]
            if ranks[i] is not None:
                ranks[i] = jnp.where(hit, float(r), ranks[i])
            ss[i] = jnp.where(hit, NEG_INF, ss[i])
    return mx, ranks


ROUTE_HEAD_GROUP = 4


def _route_kernel(t_ref, wpq_ref, sk_ref, rank2_ref, b_ref, cnt_ref, a_ref, top1, top2):
    half = PEER_QUERY_DIM // 2
    sub = 8
    tokens = pltpu.bitcast(t_ref[...], BF16)
    row = lax.broadcasted_iota(jnp.int32, (sub, tokens.shape[0]), 0)
    q = jnp.dot(tokens, wpq_ref[...], preferred_element_type=F32).astype(BF16)
    for h0 in range(0, PEER_HEADS, ROUTE_HEAD_GROUP):
        heads = range(h0, h0 + ROUTE_HEAD_GROUP)
        s1, s2 = [], []
        for h in heads:
            for side, dst in ((0, s1), (1, s2)):
                qb = q[:, (2 * h + side) * half:(2 * h + side + 1) * half]
                dst.append(lax.dot_general(sk_ref[side], qb, (((1,), (1,)), ((), ())),
                                           preferred_element_type=F32))
        _, ranks = _top_values(s1 + s2, PEER_TOPK, [top1.at[h] for h in heads] + [top2.at[h] for h in heads],
                               [False] * len(s1) + [True] * len(s2))
        rank2 = ranks[len(s1):]
        cands = []
        for h in heads:
            t1, t2 = top1.at[h], top2.at[h]
            tiles = [t1[0:1, :] + t2[0:sub, :], t1[0:1, :] + t2[sub:, :]]
            for r1 in range(1, sub):
                tile = t1[r1:r1 + 1, :] + t2[0:sub, :]
                tiles.append(jnp.where(row < PEER_TOPK // (r1 + 1), tile, NEG_INF))
            tiles.append(t1[sub:, :] + t2[0:1, :])
            cands.append(jnp.concatenate(tiles, axis=0))
        taus, _ = _top_values(cands, PEER_TOPK)
        for g, h in enumerate(heads):
            t1, t2 = top1.at[h], top2.at[h]
            cand, tau = cands[g], taus[g]
            m1 = t1[0:1, :]
            m2 = t2[0:1, :]
            z = jnp.sum(jnp.where(cand >= tau, jnp.exp(cand - (m1 + m2)), 0.0), axis=0, keepdims=True)
            first = t1[...]
            cnt = jnp.zeros_like(s1[g])
            for r2 in range(PEER_TOPK):
                passing = jnp.where((first + t2[r2:r2 + 1, :]) >= tau, first, jnp.inf)
                cnt = jnp.where(s1[g] >= jnp.min(passing, axis=0, keepdims=True), float(r2 + 1), cnt)
            rank2_ref[h] = pltpu.bitcast(rank2[g].astype(BF16), jnp.uint32)
            b_ref[h] = pltpu.bitcast((jnp.exp(s2[g] - m2) / z).astype(BF16), jnp.uint32)
            cnt_ref[h] = cnt
            a_ref[h] = jnp.exp(s1[g] - m1)


def _route(tn, w_pq, sub_keys):
    t = 2 * tn.shape[0]
    tr = ROUTE_TR
    qd = PEER_HEADS * PEER_QUERY_DIM
    out_spec = pl.BlockSpec((PEER_HEADS, PEER_N_KEYS, tr), lambda i: (0, 0, i))
    key_shape = (PEER_HEADS, PEER_N_KEYS, t)
    pair_spec = pl.BlockSpec((PEER_HEADS, PEER_N_KEYS // 2, tr), lambda i: (0, 0, i))
    pair_shape = (PEER_HEADS, PEER_N_KEYS // 2, t)
    return pl.pallas_call(
        _route_kernel,
        grid=(t // tr,),
        in_specs=[
            pl.BlockSpec((tr // 2, D_MODEL), lambda i: (i, 0)),
            pl.BlockSpec((D_MODEL, qd), lambda i: (0, 0)),
            pl.BlockSpec((2, PEER_N_KEYS, PEER_QUERY_DIM // 2), lambda i: (0, 0, 0)),
        ],
        out_specs=[pair_spec, pair_spec, out_spec, out_spec],
        out_shape=[jax.ShapeDtypeStruct(pair_shape, jnp.uint32), jax.ShapeDtypeStruct(pair_shape, jnp.uint32),
                   jax.ShapeDtypeStruct(key_shape, F32), jax.ShapeDtypeStruct(key_shape, F32)],
        scratch_shapes=[pltpu.VMEM((PEER_HEADS, PEER_TOPK, tr), F32)] * 2,
        compiler_params=_cparams(("parallel",)),
        name="route",
    )(tn, w_pq, sub_keys)


PEER_TP = 512
PEER_I1 = 8
PEER_EB = PEER_I1 * PEER_N_KEYS
NT_DIMS = (((1,), (1,)), ((), ()))
GATE_ROWS = PEER_N_KEYS


def _gate_block(first_keys, st_slot, g_slot, rank2_ref, b_ref, cnt_ref, a_ref):
    rows = GATE_ROWS
    for i in first_keys:
        for part in range(PEER_N_KEYS // rows):
            r0 = part * rows
            e0 = i * PEER_N_KEYS + r0
            for ck in range(st_slot.shape[1] // LANES):
                cols = slice(ck * LANES, (ck + 1) * LANES)
                wgt = jnp.zeros((rows, LANES), BF16)
                for h in range(PEER_HEADS):
                    cnt = cnt_ref[h, i:i + 1, cols].astype(BF16)
                    a1 = a_ref[h, i:i + 1, cols].astype(BF16)
                    rank2 = pltpu.bitcast(rank2_ref[h, r0 // 2:(r0 + rows) // 2, cols], BF16)
                    b2 = pltpu.bitcast(b_ref[h, r0 // 2:(r0 + rows) // 2, cols], BF16)
                    wgt = wgt + jnp.where(rank2 < cnt, b2, jnp.zeros_like(b2)) * a1
                pre = st_slot[e0:e0 + rows, cols].astype(BF16)
                act = 0.5 * pre * (1.0 + lax.erf(pre * (2.0 ** -0.5)))
                g_slot[e0 // 2:(e0 + rows) // 2, cols] = pltpu.bitcast(act * wgt, jnp.uint32)


def _peer_kernel(na, t_ref, u_ref, vt_ref, rank2_ref, b_ref, cnt_lo, a_lo, cnt_hi, a_hi, x1_ref, oa_ref, ob_ref,
                 acc, st0, st1, gated0, gated1):
    jj = pl.program_id(1)
    eb = PEER_EB
    st = (st0, st1)
    gated = (gated0, gated1)

    @pl.when(jj == 0)
    def _():
        for ref in (acc, st0, st1, gated0, gated1):
            ref[...] = jnp.zeros_like(ref)

    for k in range(2):
        other = 1 - k
        cnt_ref, a_ref = (cnt_lo, a_lo) if k == 0 else (cnt_hi, a_hi)
        acc[...] = jnp.dot(pltpu.bitcast(vt_ref[:, k * eb:(k + 1) * eb], BF16),
                           pltpu.bitcast(gated[k][...], BF16), preferred_element_type=F32) + acc[...]
        _gate_block(range(PEER_I1), st[other], gated[other], rank2_ref, b_ref, cnt_ref, a_ref)
        st[k][...] = lax.dot_general(pltpu.bitcast(u_ref[k * eb // 2:(k + 1) * eb // 2, :], BF16),
                                     pltpu.bitcast(t_ref[...], BF16), NT_DIMS,
                                     preferred_element_type=F32)

    last = jj == pl.num_programs(1) - 1
    for o_ref, mine in ((oa_ref, pl.program_id(0) < na), (ob_ref, pl.program_id(0) >= na)):
        @pl.when(last & mine)
        def _(o_ref=o_ref):
            o_ref[...] = x1_ref[...] + acc[...].T


def _peer(tn, u_bf, vt_bf, rank2, b, cnt, a, x1, ta):
    t = x1.shape[0]
    tp, eb = PEER_TP, PEER_EB
    assert ta % tp == 0
    na = ta // tp
    nb = 2 * u_bf.shape[0] // eb
    assert nb % 2 == 0
    pair_spec = pl.BlockSpec((PEER_HEADS, PEER_N_KEYS // 2, tp), lambda i, j: (0, 0, i))
    lo_spec = pl.BlockSpec((PEER_HEADS, PEER_I1, tp), lambda i, j: (0, jnp.maximum(2 * j - 1, 0), i))
    hi_spec = pl.BlockSpec((PEER_HEADS, PEER_I1, tp), lambda i, j: (0, jnp.minimum(2 * j, nb - 1), i))
    return pl.pallas_call(
        functools.partial(_peer_kernel, na),
        grid=(t // tp, nb // 2 + 1),
        in_specs=[
            pl.BlockSpec((tp // 2, D_MODEL), lambda i, j: (i, 0)),
            pl.BlockSpec((eb, D_MODEL), lambda i, j: (jnp.minimum(j, nb // 2 - 1), 0)),
            pl.BlockSpec((D_MODEL // 2, 2 * eb), lambda i, j: (0, jnp.maximum(j - 1, 0))),
            pair_spec, pair_spec, lo_spec, lo_spec, hi_spec, hi_spec,
            pl.BlockSpec((tp, D_MODEL), lambda i, j: (i, 0)),
        ],
        out_specs=_cat_specs(tp, D_MODEL, na, 2),
        out_shape=[jax.ShapeDtypeStruct((ta, D_MODEL), F32), jax.ShapeDtypeStruct((t - ta, D_MODEL), F32)],
        scratch_shapes=[pltpu.VMEM((D_MODEL, tp), F32), pltpu.VMEM((eb, tp), F32), pltpu.VMEM((eb, tp), F32),
                        pltpu.VMEM((eb // 2, tp), jnp.uint32), pltpu.VMEM((eb // 2, tp), jnp.uint32)],
        compiler_params=_cparams(("arbitrary", "arbitrary")),
        name="peer",
    )(tn, u_bf, vt_bf, rank2, b, cnt, a, cnt, a, x1)


PACK_TILE = 1024


def _pack_kernel(transpose, w_ref, o_ref):
    w = w_ref[...]
    if transpose:
        w = w.T
    o_ref[...] = pltpu.bitcast(w.astype(BF16), jnp.uint32)


def _pack_table(w, transpose):
    n, c = w.shape
    tile = PACK_TILE
    assert n % tile == 0 and c == tile
    if transpose:
        out_shape, out_spec = (c // 2, n), pl.BlockSpec((c // 2, tile), lambda i: (0, i))
    else:
        out_shape, out_spec = (n // 2, c), pl.BlockSpec((tile // 2, c), lambda i: (i, 0))
    return pl.pallas_call(
        functools.partial(_pack_kernel, transpose),
        grid=(n // tile,),
        in_specs=[pl.BlockSpec((tile, c), lambda i: (i, 0))],
        out_specs=out_spec,
        out_shape=jax.ShapeDtypeStruct(out_shape, jnp.uint32),
        compiler_params=_cparams(("parallel",)),
        name="pack_t" if transpose else "pack",
    )(w)


def _regroup_w_in(w_in):
    pad = jnp.zeros((D_MODEL, PROJ_COLS - COL_DT - 2 * SSD_HEADS), w_in.dtype)
    return jnp.concatenate([w_in[:, OFF_XBC:OFF_DT], w_in[:, OFF_QKV:OFF_GATE], w_in[:, :OFF_XBC],
                            w_in[:, OFF_GATE:], w_in[:, OFF_DT:OFF_QKV], pad], axis=1).astype(BF16)


def _encoder_layer(xa, xb, seqlen, g_mix, w_in, conv_w, conv_b, dt_bias, a_log, d_skip, ssd_norm_g,
                   q_norm_g, k_norm_g, rpb, w_out, g_ffn, w_pq, sub_keys, expert_u, expert_v):
    nseq = (xa.shape[0] + xb.shape[0]) // seqlen
    proj, dt_raw = _inproj(xa, xb, g_mix, _regroup_w_in(w_in))
    xbc = _conv(proj, conv_w, conv_b, nseq, seqlen)
    yf, yb = _ssd(xbc, dt_raw, dt_bias, a_log, nseq, seqlen)
    qn, kn = _qknorm(proj, q_norm_g, k_norm_g)
    yna = _natten(qn, kn, proj, rpb, nseq, seqlen)
    x1, tn = _outproj(yf, yb, xbc, proj, yna, xa, xb, w_out.astype(BF16), d_skip, ssd_norm_g, g_ffn)
    rank2, b, cnt, a = _route(tn, w_pq.astype(BF16), sub_keys.astype(BF16))
    return _peer(tn, _pack_table(expert_u, False), _pack_table(expert_v, True), rank2, b, cnt, a, x1,
                 xa.shape[0])


def kernel(x_prompt, x_sample, g_mix, w_in, conv_w, conv_b, dt_bias, a_log, d_skip, ssd_norm_g, q_norm_g,
           k_norm_g, rpb, w_out, g_ffn, w_pq, sub_keys, expert_u, expert_v):
    assert x_prompt.shape[1:] == x_sample.shape[1:]
    seqlen = x_prompt.shape[1]
    xa = x_prompt.reshape(-1, D_MODEL)
    xb = x_sample.reshape(-1, D_MODEL)
    for i in range(g_mix.shape[0]):
        xa, xb = _encoder_layer(xa, xb, seqlen, g_mix[i], w_in[i], conv_w[i], conv_b[i], dt_bias[i], a_log[i],
                                d_skip[i], ssd_norm_g[i], q_norm_g[i], k_norm_g[i], rpb[i], w_out[i], g_ffn[i],
                                w_pq[i], sub_keys[i], expert_u[i], expert_v[i])
    return (xa.reshape(x_prompt.shape), xb.reshape(x_sample.shape))
```

```python
import functools

import jax
import jax.numpy as jnp
from jax import lax
from jax.experimental import pallas as pl
from jax.experimental.pallas import tpu as pltpu

F32 = jnp.float32
BF16 = jnp.bfloat16

D_MODEL = 1024
GRID_W = 64
EPS = 1e-6
D_INNER = 2048
SSD_HEAD_DIM = 64
SSD_HEADS = 32
SSD_GROUPS = 4
SSD_STATE = 128
CONV_W = 5
CHUNK = 128
CONV_DIM = D_INNER + 2 * SSD_GROUPS * SSD_STATE
NA_HEADS = 16
NA_HEAD_DIM = 64
NA_WIDTH = 1024
WIN_ROWS = 8
WIN_COLS = 16
PEER_HEADS = 8
PEER_N_KEYS = 128
PEER_TOPK = 16
PEER_QUERY_DIM = 256
OFF_XBC = D_INNER
OFF_DT = OFF_XBC + CONV_DIM
OFF_QKV = OFF_DT + 2 * SSD_HEADS
OFF_GATE = OFF_QKV + 3 * NA_WIDTH

LANES = 128
MXU_COLS = 256
COL_QKV = CONV_DIM
COL_Z = COL_QKV + 3 * NA_WIDTH
COL_GATE = COL_Z + D_INNER
COL_DT = COL_GATE + 2 * D_MODEL
INPROJ_TN = 6 * MXU_COLS
PROJ_COLS = -(-(COL_DT + LANES) // INPROJ_TN) * INPROJ_TN
DT_LOCAL = COL_DT - (PROJ_COLS - INPROJ_TN)

VMEM_LIMIT = 56 * 1024 * 1024


def _cparams(sem, flags=None):
    return pltpu.CompilerParams(dimension_semantics=sem, vmem_limit_bytes=VMEM_LIMIT, flags=flags)


def _rms(x, g):
    return x * lax.rsqrt(jnp.mean(x * x, axis=-1, keepdims=True) + EPS) * g


INPROJ_TM = 1024


def _cat_specs(tm, width, na, grid_rank):
    if grid_rank == 1:
        return [pl.BlockSpec((tm, width), lambda i: (jnp.minimum(i, na - 1), 0)),
                pl.BlockSpec((tm, width), lambda i: (jnp.maximum(i - na, 0), 0))]
    return [pl.BlockSpec((tm, width), lambda i, j: (jnp.minimum(i, na - 1), 0)),
            pl.BlockSpec((tm, width), lambda i, j: (jnp.maximum(i - na, 0), 0))]


def _cat_tile(na, a_ref, b_ref):
    return jnp.where(pl.program_id(0) < na, a_ref[...], b_ref[...])


def _inproj_kernel(na, xa_ref, xb_ref, g_ref, w_ref, o_ref, dt_ref, h_scr):
    j = pl.program_id(1)

    @pl.when(j == 0)
    def _():
        h_scr[...] = _rms(_cat_tile(na, xa_ref, xb_ref), g_ref[...]).astype(BF16)

    acc = jnp.dot(h_scr[...], w_ref[...], preferred_element_type=F32)
    o_ref[...] = acc.astype(BF16)

    @pl.when(j == pl.num_programs(1) - 1)
    def _():
        dt_ref[...] = acc[:, DT_LOCAL:DT_LOCAL + LANES]


def _inproj(xa, xb, g_mix, w_cat):
    t = xa.shape[0] + xb.shape[0]
    tm, tn = INPROJ_TM, INPROJ_TN
    assert xa.shape[0] % tm == 0 and xb.shape[0] % tm == 0 and PROJ_COLS % tn == 0 and COL_DT >= PROJ_COLS - tn
    na = xa.shape[0] // tm
    return pl.pallas_call(
        functools.partial(_inproj_kernel, na),
        grid=(t // tm, PROJ_COLS // tn),
        in_specs=_cat_specs(tm, D_MODEL, na, 2) + [
            pl.BlockSpec((1, D_MODEL), lambda i, j: (0, 0)),
            pl.BlockSpec((D_MODEL, tn), lambda i, j: (0, j)),
        ],
        out_specs=[
            pl.BlockSpec((tm, tn), lambda i, j: (i, j)),
            pl.BlockSpec((tm, LANES), lambda i, j: (i, 0)),
        ],
        out_shape=[
            jax.ShapeDtypeStruct((t, PROJ_COLS), BF16),
            jax.ShapeDtypeStruct((t, LANES), F32),
        ],
        scratch_shapes=[pltpu.VMEM((tm, D_MODEL), BF16)],
        compiler_params=_cparams(("parallel", "arbitrary")),
        name="inproj",
    )(xa, xb, g_mix.reshape(1, D_MODEL), w_cat)


CONV_TL = 512
CONV_CB = 1024
HALO = 16
PAD = 8


def _conv_kernel(cur_ref, prev_ref, next_ref, w_ref, b_ref, o_ref, ext):
    i = pl.program_id(1)
    tl = cur_ref.shape[0]
    prev = prev_ref[...].astype(F32)[HALO - PAD:, :]
    nxt = next_ref[...].astype(F32)[:PAD, :]
    ext[0:PAD, :] = jnp.where(i > 0, prev, 0.0)
    ext[PAD:PAD + tl, :] = cur_ref[...].astype(F32)
    ext[PAD + tl:, :] = jnp.where(i < pl.num_programs(1) - 1, nxt, 0.0)
    acc = jnp.broadcast_to(b_ref[...], o_ref.shape)
    for k in range(CONV_W):
        acc = acc + w_ref[k:k + 1, :] * ext[pl.ds(PAD - CONV_W // 2 + k, tl), :]
    o_ref[...] = (acc * jax.nn.sigmoid(acc)).astype(BF16)


def _conv(proj, conv_w, conv_b, nseq, seqlen):
    t = proj.shape[0]
    tl, cb = CONV_TL, CONV_CB
    nl = seqlen // tl
    assert seqlen % tl == 0 and CONV_DIM % cb == 0
    return pl.pallas_call(
        _conv_kernel,
        grid=(nseq, nl, CONV_DIM // cb),
        in_specs=[
            pl.BlockSpec((tl, cb), lambda s, i, c: (s * nl + i, c)),
            pl.BlockSpec((HALO, cb), lambda s, i, c: (jnp.maximum((s * nl + i) * (tl // HALO) - 1, 0), c)),
            pl.BlockSpec((HALO, cb),
                         lambda s, i, c: (jnp.minimum((s * nl + i + 1) * (tl // HALO), t // HALO - 1), c)),
            pl.BlockSpec((CONV_W, cb), lambda s, i, c: (0, c)),
            pl.BlockSpec((1, cb), lambda s, i, c: (0, c)),
        ],
        out_specs=pl.BlockSpec((tl, cb), lambda s, i, c: (s * nl + i, c)),
        out_shape=jax.ShapeDtypeStruct((t, CONV_DIM), BF16),
        scratch_shapes=[pltpu.VMEM((tl + 2 * PAD, cb), F32)],
        compiler_params=_cparams(("parallel", "parallel", "parallel")),
        name="conv",
    )(proj, proj, proj, conv_w, conv_b.reshape(1, CONV_DIM))


def _cumsum_rows(x):
    row = lax.broadcasted_iota(jnp.int32, x.shape, 0)
    d = 1
    while d < x.shape[0]:
        x = x + jnp.where(row >= d, pltpu.roll(x, d, 0), 0.0)
        d *= 2
    return x


def _ssd_direction(reverse, xs_ref, b_ref, c_ref, dt_ref, dtb_ref, alog_ref, y_ref, st):
    row = lax.broadcasted_iota(jnp.int32, (CHUNK, CHUNK), 0)
    col = lax.broadcasted_iota(jnp.int32, (CHUNK, CHUNK), 1)
    lane1 = lax.broadcasted_iota(jnp.int32, (1, LANES), 1)
    dt_pre = dt_ref[...] + dtb_ref[...]
    dt = jnp.maximum(dt_pre, 0.0) + jnp.log1p(jnp.exp(-jnp.abs(dt_pre)))
    dta = dt * (-jnp.exp(alog_ref[...]))
    cs = _cumsum_rows(dta)
    total = cs[CHUNK - 1:CHUNK, :]
    if reverse:
        cum = total - cs + dta
        wexp = cs - dta
        tri = col >= row
    else:
        cum = cs
        wexp = total - cs
        tri = row >= col
    w = jnp.exp(wexp) * dt
    cum_t = cum.T
    dt_t = dt.T
    w_t = w.T
    dec = jnp.exp(total)
    base = SSD_HEADS if reverse else 0
    pairs_per_group = SSD_HEADS // SSD_GROUPS // 2
    for g in range(SSD_GROUPS):
        bg = b_ref[:, g * SSD_STATE:(g + 1) * SSD_STATE]
        cg = c_ref[:, g * SSD_STATE:(g + 1) * SSD_STATE]
        gmat = lax.dot_general(cg, bg, (((1,), (1,)), ((), ())), preferred_element_type=F32)
        bg_t = bg.astype(F32).T
        cg32 = cg.astype(F32)
        for hp in range(pairs_per_group):
            j = g * pairs_per_group + hp
            sl = slice(j * LANES, (j + 1) * LANES)
            xs_pair = xs_ref[:, sl].astype(F32)
            st_pair = st[:, sl]
            y_acc = jnp.zeros((CHUNK, LANES), F32)
            s_acc = jnp.zeros((SSD_STATE, LANES), F32)
            dec_pair = jnp.zeros((1, LANES), F32)
            for e in range(2):
                ln = base + 2 * j + e
                in_head = (col // SSD_HEAD_DIM) == e
                colb = jnp.broadcast_to(cum[:, ln:ln + 1], (CHUNK, CHUNK))
                decay = jnp.where(tri, jnp.exp(colb - cum_t[ln:ln + 1, :]), 0.0)
                m = gmat * decay * dt_t[ln:ln + 1, :]
                ce = cg32 * jnp.exp(colb)
                lhs = jnp.concatenate([m, ce], axis=1).astype(BF16)
                xs_m = jnp.where(in_head, xs_pair, 0.0).astype(BF16)
                st_m = jnp.where(in_head, st_pair, 0.0).astype(BF16)
                rhs = jnp.concatenate([xs_m, st_m], axis=0)
                y_acc = y_acc + jnp.dot(lhs, rhs, preferred_element_type=F32)
                btw = (bg_t * w_t[ln:ln + 1, :]).astype(BF16)
                s_acc = s_acc + jnp.dot(btw, xs_m, preferred_element_type=F32)
                dec_pair = jnp.where((lane1 // SSD_HEAD_DIM) == e,
                                     jnp.broadcast_to(dec[:, ln:ln + 1], (1, LANES)), dec_pair)
            y_ref[:, sl] = y_acc.astype(BF16)
            st[:, sl] = st_pair * dec_pair + s_acc


def _ssd_kernel(xs_f, b_f, c_f, dt_f, xs_b, b_b, c_b, dt_b, dtb_ref, alog_ref, yf_ref, yb_ref, st_f, st_b):
    @pl.when(pl.program_id(1) == 0)
    def _():
        st_f[...] = jnp.zeros_like(st_f)
        st_b[...] = jnp.zeros_like(st_b)

    _ssd_direction(False, xs_f, b_f, c_f, dt_f, dtb_ref, alog_ref, yf_ref, st_f)
    _ssd_direction(True, xs_b, b_b, c_b, dt_b, dtb_ref, alog_ref, yb_ref, st_b)


def _ssd(xbc, dt_raw, dt_bias, a_log, nseq, seqlen):
    t = xbc.shape[0]
    nc = seqlen // CHUNK
    bn = SSD_GROUPS * SSD_STATE
    fwd = lambda s, c: s * nc + c
    bwd = lambda s, c: s * nc + (nc - 1 - c)

    def specs(rb):
        return [
            pl.BlockSpec((CHUNK, D_INNER), lambda s, c: (rb(s, c), 0)),
            pl.BlockSpec((CHUNK, bn), lambda s, c: (rb(s, c), D_INNER // bn)),
            pl.BlockSpec((CHUNK, bn), lambda s, c: (rb(s, c), D_INNER // bn + 1)),
        ]

    dt_spec = lambda rb: pl.BlockSpec((CHUNK, LANES), lambda s, c: (rb(s, c), 0))
    row_spec = pl.BlockSpec((1, LANES), lambda s, c: (0, 0))
    pad = jnp.zeros((LANES - 2 * SSD_HEADS,), F32)
    dtb = jnp.concatenate([dt_bias.reshape(-1), pad]).reshape(1, LANES)
    alog = jnp.concatenate([a_log.reshape(-1), pad]).reshape(1, LANES)
    return pl.pallas_call(
        _ssd_kernel,
        grid=(nseq, nc),
        in_specs=specs(fwd) + [dt_spec(fwd)] + specs(bwd) + [dt_spec(bwd)] + [row_spec, row_spec],
        out_specs=[
            pl.BlockSpec((CHUNK, D_INNER), lambda s, c: (fwd(s, c), 0)),
            pl.BlockSpec((CHUNK, D_INNER), lambda s, c: (bwd(s, c), 0)),
        ],
        out_shape=[jax.ShapeDtypeStruct((t, D_INNER), BF16)] * 2,
        scratch_shapes=[pltpu.VMEM((SSD_STATE, D_INNER), F32)] * 2,
        compiler_params=_cparams(("parallel", "arbitrary")),
        name="ssd",
    )(xbc, xbc, xbc, dt_raw, xbc, xbc, xbc, dt_raw, dtb, alog)


QKN_TM = 512


def _qknorm_kernel(q_ref, k_ref, gq_ref, gk_ref, qo_ref, ko_ref):
    r = lax.broadcasted_iota(jnp.int32, (LANES, LANES), 0) // NA_HEAD_DIM
    c = lax.broadcasted_iota(jnp.int32, (LANES, LANES), 1) // NA_HEAD_DIM
    same_head = jnp.where(r == c, 1.0, 0.0).astype(BF16)
    for src, g_ref, dst, scale in ((q_ref, gq_ref, qo_ref, NA_HEAD_DIM ** -0.5), (k_ref, gk_ref, ko_ref, None)):
        for p in range(NA_WIDTH // LANES):
            sl = slice(p * LANES, (p + 1) * LANES)
            x = src[:, sl].astype(F32)
            sq = x * x
            hi = sq.astype(BF16)
            lo = (sq - hi.astype(F32)).astype(BF16)
            ss = (jnp.dot(hi, same_head, preferred_element_type=F32)
                  + jnp.dot(lo, same_head, preferred_element_type=F32))
            y = x * lax.rsqrt(ss / NA_HEAD_DIM + EPS) * g_ref[...]
            if scale is not None:
                y = y * scale
            dst[:, sl] = y.astype(BF16)


def _qknorm(proj, q_norm_g, k_norm_g):
    t = proj.shape[0]
    tm = QKN_TM
    reps = LANES // NA_HEAD_DIM
    gq = jnp.tile(q_norm_g, reps).reshape(1, LANES)
    gk = jnp.tile(k_norm_g, reps).reshape(1, LANES)
    qcol = COL_QKV // NA_WIDTH
    return pl.pallas_call(
        _qknorm_kernel,
        grid=(t // tm,),
        in_specs=[
            pl.BlockSpec((tm, NA_WIDTH), lambda i: (i, qcol)),
            pl.BlockSpec((tm, NA_WIDTH), lambda i: (i, qcol + 1)),
            pl.BlockSpec((1, LANES), lambda i: (0, 0)),
            pl.BlockSpec((1, LANES), lambda i: (0, 0)),
        ],
        out_specs=[pl.BlockSpec((tm, NA_WIDTH), lambda i: (i, 0))] * 2,
        out_shape=[jax.ShapeDtypeStruct((t, NA_WIDTH), BF16)] * 2,
        compiler_params=_cparams(("parallel",)),
        name="qknorm",
    )(proj, proj, gq, gk)


def _natten_kernel(*refs):
    q_ref = refs[0]
    k_refs = refs[1:1 + WIN_ROWS]
    v_refs = refs[1 + WIN_ROWS:1 + 2 * WIN_ROWS]
    bias_ref = refs[1 + 2 * WIN_ROWS]
    o_ref = refs[2 + 2 * WIN_ROWS]
    first = lax.broadcasted_iota(jnp.int32, (GRID_W, LANES), 1) < NA_HEAD_DIM
    for p in range(NA_WIDTH // LANES):
        sl = slice(p * LANES, (p + 1) * LANES)
        qp = q_ref[:, sl].astype(F32)
        qrows = jnp.concatenate([jnp.where(first, qp, 0.0), jnp.where(first, 0.0, qp)], axis=0).astype(BF16)
        kp = jnp.concatenate([r[:, sl] for r in k_refs], axis=0)
        vp = jnp.concatenate([r[:, sl] for r in v_refs], axis=0)
        st = lax.dot_general(kp, qrows, (((1,), (1,)), ((), ())), preferred_element_type=F32)
        st = st + bias_ref[0, p]
        m = jnp.max(st, axis=0, keepdims=True)
        pe = jnp.exp(st - m)
        inv = 1.0 / jnp.sum(pe, axis=0, keepdims=True)
        pn = (pe * inv).astype(BF16)
        o2 = lax.dot_general(pn, vp, (((0,), (0,)), ((), ())), preferred_element_type=F32)
        o_ref[:, sl] = jnp.where(first, o2[:GRID_W], o2[GRID_W:]).astype(BF16)


def _na_bias_table(rpb):
    cols = jnp.arange(GRID_W)
    c0 = jnp.clip(cols - WIN_COLS // 2, 0, GRID_W - WIN_COLS)
    j = jnp.arange(GRID_W)
    valid = (j[None, :] >= c0[:, None]) & (j[None, :] < c0[:, None] + WIN_COLS)
    k = jnp.arange(2 * WIN_COLS - 1)
    hp = LANES // NA_HEAD_DIM
    x = jnp.arange(LANES)
    onehot = ((x[None, None, None, :] // GRID_W == jnp.arange(hp)[:, None, None, None])
              & (k[None, :, None, None] == j[None, None, :, None] - x[None, None, None, :] % GRID_W + (WIN_COLS - 1))
              ).astype(F32)
    toep = jnp.einsum("pgrk,gkjx->prjx", rpb.astype(F32).reshape(NA_HEADS // hp, hp, *rpb.shape[1:]), onehot,
                      precision=lax.Precision.HIGHEST)
    valid_x = jnp.tile(valid.T, (1, hp))
    toep = jnp.where(valid_x[None, None], toep, -jnp.inf)
    per_delta = jnp.stack([toep[:, WIN_ROWS - 1 - d:2 * WIN_ROWS - 1 - d] for d in range(WIN_ROWS)], axis=0)
    return per_delta.reshape(WIN_ROWS, NA_HEADS // hp, WIN_ROWS * GRID_W, LANES)


def _natten(qn, kn, proj, rpb, nseq, seqlen):
    t = qn.shape[0]
    rows = seqlen // GRID_W
    assert rows >= WIN_ROWS
    r0 = lambda r: jnp.clip(r - WIN_ROWS // 2, 0, rows - WIN_ROWS)
    vcol = COL_QKV // NA_WIDTH + 2
    tbl = _na_bias_table(rpb)

    def win_spec(i, colblk):
        return pl.BlockSpec((GRID_W, NA_WIDTH), lambda s, r: (s * rows + r0(r) + i, colblk))

    return pl.pallas_call(
        _natten_kernel,
        grid=(nseq, rows),
        in_specs=([pl.BlockSpec((GRID_W, NA_WIDTH), lambda s, r: (s * rows + r, 0))]
                  + [win_spec(i, 0) for i in range(WIN_ROWS)]
                  + [win_spec(i, vcol) for i in range(WIN_ROWS)]
                  + [pl.BlockSpec((1,) + tbl.shape[1:], lambda s, r: (r - r0(r), 0, 0, 0))]),
        out_specs=pl.BlockSpec((GRID_W, NA_WIDTH), lambda s, r: (s * rows + r, 0)),
        out_shape=jax.ShapeDtypeStruct((t, NA_WIDTH), BF16),
        compiler_params=_cparams(("parallel", "arbitrary")),
        name="natten",
    )(qn, *([kn] * WIN_ROWS), *([proj] * WIN_ROWS), tbl)


OUTPROJ_TM = 512


def _outproj_kernel(na, yf_ref, yb_ref, xs_ref, z_ref, g1_ref, g2_ref, yna_ref, xa_ref, xb_ref, w1_ref, w2_ref,
                    dsk_ref, ng_ref, gffn_ref, x1_ref, t_ref):
    y = yf_ref[...].astype(F32) + yb_ref[...].astype(F32) + dsk_ref[...] * xs_ref[...].astype(F32)
    y = y * jax.nn.silu(z_ref[...].astype(F32))
    yn = _rms(y, ng_ref[...]).astype(BF16)
    br_ssd = jnp.dot(yn, w1_ref[...], preferred_element_type=F32)
    br_na = jnp.dot(yna_ref[...], w2_ref[...], preferred_element_type=F32)
    mix = (jax.nn.sigmoid(g1_ref[...].astype(F32)) * br_ssd
           + jax.nn.sigmoid(g2_ref[...].astype(F32)) * br_na)
    x1 = _cat_tile(na, xa_ref, xb_ref) + mix
    x1_ref[...] = x1
    t_ref[...] = pltpu.bitcast(_rms(x1, gffn_ref[...]).astype(BF16), jnp.uint32)


def _outproj(yf, yb, xbc, proj, yna, xa, xb, w_out, d_skip, ssd_norm_g, g_ffn):
    t = xa.shape[0] + xb.shape[0]
    tm = OUTPROJ_TM
    na = xa.shape[0] // tm
    tok = lambda w, cb: pl.BlockSpec((tm, w), lambda i: (i, cb))
    row = lambda w: pl.BlockSpec((1, w), lambda i: (0, 0))
    dsk = jnp.repeat(d_skip, SSD_HEAD_DIM).reshape(1, D_INNER)
    return pl.pallas_call(
        functools.partial(_outproj_kernel, na),
        grid=(t // tm,),
        in_specs=[
            tok(D_INNER, 0), tok(D_INNER, 0), tok(D_INNER, 0),
            tok(D_INNER, COL_Z // D_INNER),
            tok(D_MODEL, COL_GATE // D_MODEL), tok(D_MODEL, COL_GATE // D_MODEL + 1),
            tok(NA_WIDTH, 0), *_cat_specs(tm, D_MODEL, na, 1),
            pl.BlockSpec((D_INNER, D_MODEL), lambda i: (0, 0)),
            pl.BlockSpec((NA_WIDTH, D_MODEL), lambda i: (D_INNER // NA_WIDTH, 0)),
            row(D_INNER), row(D_INNER), row(D_MODEL),
        ],
        out_specs=[tok(D_MODEL, 0), pl.BlockSpec((tm // 2, D_MODEL), lambda i: (i, 0))],
        out_shape=[jax.ShapeDtypeStruct((t, D_MODEL), F32), jax.ShapeDtypeStruct((t // 2, D_MODEL), jnp.uint32)],
        compiler_params=_cparams(("parallel",)),
        name="outproj",
    )(yf, yb, xbc, proj, proj, proj, yna, xa, xb, w_out, w_out, dsk,
      ssd_norm_g.reshape(1, D_INNER), g_ffn.reshape(1, D_MODEL))


ROUTE_TR = 256
NEG_INF = float("-inf")
RANK_NONE = 64.0


def _top_values(ss, k, out_refs=None, want_rank=None):
    n = len(ss)
    ss = list(ss)
    out_refs = out_refs or [None] * n
    want_rank = want_rank or [False] * n
    ranks = [jnp.full(s.shape, RANK_NONE, F32) if w else None for s, w in zip(ss, want_rank)]
    mx = [None] * n
    for r in range(k):
        for i in range(n):
            mx[i] = jnp.max(ss[i], axis=0, keepdims=True)
            if out_refs[i] is not None:
                out_refs[i][r:r + 1, :] = mx[i]
            hit = ss[i] == mx[i]
            if ranks[i] is not None:
                ranks[i] = jnp.where(hit, float(r), ranks[i])
            ss[i] = jnp.where(hit, NEG_INF, ss[i])
    return mx, ranks


ROUTE_HEAD_GROUP = 4


def _route_kernel(t_ref, wpq_ref, sk_ref, rank2_ref, b_ref, cnt_ref, a_ref, top1, top2):
    half = PEER_QUERY_DIM // 2
    sub = 8
    tokens = pltpu.bitcast(t_ref[...], BF16)
    row = lax.broadcasted_iota(jnp.int32, (sub, tokens.shape[0]), 0)
    q = jnp.dot(tokens, wpq_ref[...], preferred_element_type=F32).astype(BF16)
    for h0 in range(0, PEER_HEADS, ROUTE_HEAD_GROUP):
        heads = range(h0, h0 + ROUTE_HEAD_GROUP)
        s1, s2 = [], []
        for h in heads:
            for side, dst in ((0, s1), (1, s2)):
                qb = q[:, (2 * h + side) * half:(2 * h + side + 1) * half]
                dst.append(lax.dot_general(sk_ref[side], qb, (((1,), (1,)), ((), ())),
                                           preferred_element_type=F32))
        _, ranks = _top_values(s1 + s2, PEER_TOPK, [top1.at[h] for h in heads] + [top2.at[h] for h in heads],
                               [False] * len(s1) + [True] * len(s2))
        rank2 = ranks[len(s1):]
        cands = []
        for h in heads:
            t1, t2 = top1.at[h], top2.at[h]
            tiles = [t1[0:1, :] + t2[0:sub, :], t1[0:1, :] + t2[sub:, :]]
            for r1 in range(1, sub):
                tile = t1[r1:r1 + 1, :] + t2[0:sub, :]
                tiles.append(jnp.where(row < PEER_TOPK // (r1 + 1), tile, NEG_INF))
            tiles.append(t1[sub:, :] + t2[0:1, :])
            cands.append(jnp.concatenate(tiles, axis=0))
        taus, _ = _top_values(cands, PEER_TOPK)
        for g, h in enumerate(heads):
            t1, t2 = top1.at[h], top2.at[h]
            cand, tau = cands[g], taus[g]
            m1 = t1[0:1, :]
            m2 = t2[0:1, :]
            z = jnp.sum(jnp.where(cand >= tau, jnp.exp(cand - (m1 + m2)), 0.0), axis=0, keepdims=True)
            first = t1[...]
            cnt = jnp.zeros_like(s1[g])
            for r2 in range(PEER_TOPK):
                passing = jnp.where((first + t2[r2:r2 + 1, :]) >= tau, first, jnp.inf)
                cnt = jnp.where(s1[g] >= jnp.min(passing, axis=0, keepdims=True), float(r2 + 1), cnt)
            rank2_ref[h] = pltpu.bitcast(rank2[g].astype(BF16), jnp.uint32)
            b_ref[h] = pltpu.bitcast((jnp.exp(s2[g] - m2) / z).astype(BF16), jnp.uint32)
            cnt_ref[h] = cnt
            a_ref[h] = jnp.exp(s1[g] - m1)


def _route(tn, w_pq, sub_keys):
    t = 2 * tn.shape[0]
    tr = ROUTE_TR
    qd = PEER_HEADS * PEER_QUERY_DIM
    out_spec = pl.BlockSpec((PEER_HEADS, PEER_N_KEYS, tr), lambda i: (0, 0, i))
    key_shape = (PEER_HEADS, PEER_N_KEYS, t)
    pair_spec = pl.BlockSpec((PEER_HEADS, PEER_N_KEYS // 2, tr), lambda i: (0, 0, i))
    pair_shape = (PEER_HEADS, PEER_N_KEYS // 2, t)
    return pl.pallas_call(
        _route_kernel,
        grid=(t // tr,),
        in_specs=[
            pl.BlockSpec((tr // 2, D_MODEL), lambda i: (i, 0)),
            pl.BlockSpec((D_MODEL, qd), lambda i: (0, 0)),
            pl.BlockSpec((2, PEER_N_KEYS, PEER_QUERY_DIM // 2), lambda i: (0, 0, 0)),
        ],
        out_specs=[pair_spec, pair_spec, out_spec, out_spec],
        out_shape=[jax.ShapeDtypeStruct(pair_shape, jnp.uint32), jax.ShapeDtypeStruct(pair_shape, jnp.uint32),
                   jax.ShapeDtypeStruct(key_shape, F32), jax.ShapeDtypeStruct(key_shape, F32)],
        scratch_shapes=[pltpu.VMEM((PEER_HEADS, PEER_TOPK, tr), F32)] * 2,
        compiler_params=_cparams(("parallel",)),
        name="route",
    )(tn, w_pq, sub_keys)


PEER_TP = 512
PEER_I1 = 8
PEER_EB = PEER_I1 * PEER_N_KEYS
NT_DIMS = (((1,), (1,)), ((), ()))
GATE_ROWS = PEER_N_KEYS


def _gate_block(first_keys, st_slot, g_slot, rank2_ref, b_ref, cnt_ref, a_ref):
    rows = GATE_ROWS
    for i in first_keys:
        for part in range(PEER_N_KEYS // rows):
            r0 = part * rows
            e0 = i * PEER_N_KEYS + r0
            for ck in range(st_slot.shape[1] // LANES):
                cols = slice(ck * LANES, (ck + 1) * LANES)
                wgt = jnp.zeros((rows, LANES), BF16)
                for h in range(PEER_HEADS):
                    cnt = cnt_ref[h, i:i + 1, cols].astype(BF16)
                    a1 = a_ref[h, i:i + 1, cols].astype(BF16)
                    rank2 = pltpu.bitcast(rank2_ref[h, r0 // 2:(r0 + rows) // 2, cols], BF16)
                    b2 = pltpu.bitcast(b_ref[h, r0 // 2:(r0 + rows) // 2, cols], BF16)
                    wgt = wgt + jnp.where(rank2 < cnt, b2, jnp.zeros_like(b2)) * a1
                pre = st_slot[e0:e0 + rows, cols].astype(BF16)
                act = 0.5 * pre * (1.0 + lax.erf(pre * (2.0 ** -0.5)))
                g_slot[e0 // 2:(e0 + rows) // 2, cols] = pltpu.bitcast(act * wgt, jnp.uint32)


def _peer_kernel(na, t_ref, u_ref, vt_ref, rank2_ref, b_ref, cnt_lo, a_lo, cnt_hi, a_hi, x1_ref, oa_ref, ob_ref,
                 acc, st0, st1, gated0, gated1):
    jj = pl.program_id(1)
    last = pl.num_programs(1) - 1
    eb = PEER_EB
    st = (st0, st1)
    gated = (gated0, gated1)

    def accumulate(k):
        acc[...] = jnp.dot(pltpu.bitcast(vt_ref[:, k * eb:(k + 1) * eb], BF16),
                           pltpu.bitcast(gated[k][...], BF16), preferred_element_type=F32) + acc[...]

    def gate(k):
        cnt_ref, a_ref = (cnt_lo, a_lo) if k == 0 else (cnt_hi, a_hi)
        _gate_block(range(PEER_I1), st[1 - k], gated[1 - k], rank2_ref, b_ref, cnt_ref, a_ref)

    def scores(k):
        st[k][...] = lax.dot_general(pltpu.bitcast(u_ref[k * eb // 2:(k + 1) * eb // 2, :], BF16),
                                     pltpu.bitcast(t_ref[...], BF16), NT_DIMS, preferred_element_type=F32)

    @pl.when(jj == 0)
    def _():
        acc[...] = jnp.zeros_like(acc)
        scores(0)
        gate(1)
        scores(1)

    @pl.when((jj > 0) & (jj < last))
    def _():
        for k in range(2):
            accumulate(k)
            gate(k)
            scores(k)

    @pl.when(jj == last)
    def _():
        accumulate(0)
        gate(0)
        accumulate(1)

    for o_ref, mine in ((oa_ref, pl.program_id(0) < na), (ob_ref, pl.program_id(0) >= na)):
        @pl.when((jj == last) & mine)
        def _(o_ref=o_ref):
            o_ref[...] = x1_ref[...] + acc[...].T


def _peer(tn, u_bf, vt_bf, rank2, b, cnt, a, x1, ta):
    t = x1.shape[0]
    tp, eb = PEER_TP, PEER_EB
    assert ta % tp == 0
    na = ta // tp
    nb = 2 * u_bf.shape[0] // eb
    assert nb % 2 == 0
    pair_spec = pl.BlockSpec((PEER_HEADS, PEER_N_KEYS // 2, tp), lambda i, j: (0, 0, i))
    lo_spec = pl.BlockSpec((PEER_HEADS, PEER_I1, tp), lambda i, j: (0, jnp.maximum(2 * j - 1, 0), i))
    hi_spec = pl.BlockSpec((PEER_HEADS, PEER_I1, tp), lambda i, j: (0, jnp.minimum(2 * j, nb - 1), i))
    return pl.pallas_call(
        functools.partial(_peer_kernel, na),
        grid=(t // tp, nb // 2 + 1),
        in_specs=[
            pl.BlockSpec((tp // 2, D_MODEL), lambda i, j: (i, 0)),
            pl.BlockSpec((eb, D_MODEL), lambda i, j: (jnp.minimum(j, nb // 2 - 1), 0)),
            pl.BlockSpec((D_MODEL // 2, 2 * eb), lambda i, j: (0, jnp.maximum(j - 1, 0))),
            pair_spec, pair_spec, lo_spec, lo_spec, hi_spec, hi_spec,
            pl.BlockSpec((tp, D_MODEL), lambda i, j: (i, 0)),
        ],
        out_specs=_cat_specs(tp, D_MODEL, na, 2),
        out_shape=[jax.ShapeDtypeStruct((ta, D_MODEL), F32), jax.ShapeDtypeStruct((t - ta, D_MODEL), F32)],
        scratch_shapes=[pltpu.VMEM((D_MODEL, tp), F32), pltpu.VMEM((eb, tp), F32), pltpu.VMEM((eb, tp), F32),
                        pltpu.VMEM((eb // 2, tp), jnp.uint32), pltpu.VMEM((eb // 2, tp), jnp.uint32)],
        compiler_params=_cparams(("arbitrary", "arbitrary")),
        name="peer",
    )(tn, u_bf, vt_bf, rank2, b, cnt, a, cnt, a, x1)


PACK_TILE = 1024


def _pack_kernel(transpose, w_ref, o_ref):
    w = w_ref[...]
    if transpose:
        w = w.T
    o_ref[...] = pltpu.bitcast(w.astype(BF16), jnp.uint32)


def _pack_table(w, transpose):
    n, c = w.shape
    tile = PACK_TILE
    assert n % tile == 0 and c == tile
    if transpose:
        out_shape, out_spec = (c // 2, n), pl.BlockSpec((c // 2, tile), lambda i: (0, i))
    else:
        out_shape, out_spec = (n // 2, c), pl.BlockSpec((tile // 2, c), lambda i: (i, 0))
    return pl.pallas_call(
        functools.partial(_pack_kernel, transpose),
        grid=(n // tile,),
        in_specs=[pl.BlockSpec((tile, c), lambda i: (i, 0))],
        out_specs=out_spec,
        out_shape=jax.ShapeDtypeStruct(out_shape, jnp.uint32),
        compiler_params=_cparams(("parallel",)),
        name="pack_t" if transpose else "pack",
    )(w)


def _regroup_w_in(w_in):
    pad = jnp.zeros((D_MODEL, PROJ_COLS - COL_DT - 2 * SSD_HEADS), w_in.dtype)
    return jnp.concatenate([w_in[:, OFF_XBC:OFF_DT], w_in[:, OFF_QKV:OFF_GATE], w_in[:, :OFF_XBC],
                            w_in[:, OFF_GATE:], w_in[:, OFF_DT:OFF_QKV], pad], axis=1).astype(BF16)


def _encoder_layer(xa, xb, seqlen, g_mix, w_in, conv_w, conv_b, dt_bias, a_log, d_skip, ssd_norm_g,
                   q_norm_g, k_norm_g, rpb, w_out, g_ffn, w_pq, sub_keys, expert_u, expert_v):
    nseq = (xa.shape[0] + xb.shape[0]) // seqlen
    proj, dt_raw = _inproj(xa, xb, g_mix, _regroup_w_in(w_in))
    xbc = _conv(proj, conv_w, conv_b, nseq, seqlen)
    yf, yb = _ssd(xbc, dt_raw, dt_bias, a_log, nseq, seqlen)
    qn, kn = _qknorm(proj, q_norm_g, k_norm_g)
    yna = _natten(qn, kn, proj, rpb, nseq, seqlen)
    x1, tn = _outproj(yf, yb, xbc, proj, yna, xa, xb, w_out.astype(BF16), d_skip, ssd_norm_g, g_ffn)
    rank2, b, cnt, a = _route(tn, w_pq.astype(BF16), sub_keys.astype(BF16))
    return _peer(tn, _pack_table(expert_u, False), _pack_table(expert_v, True), rank2, b, cnt, a, x1,
                 xa.shape[0])


def kernel(x_prompt, x_sample, g_mix, w_in, conv_w, conv_b, dt_bias, a_log, d_skip, ssd_norm_g, q_norm_g,
           k_norm_g, rpb, w_out, g_ffn, w_pq, sub_keys, expert_u, expert_v):
    assert x_prompt.shape[1:] == x_sample.shape[1:]
    seqlen = x_prompt.shape[1]
    xa = x_prompt.reshape(-1, D_MODEL)
    xb = x_sample.reshape(-1, D_MODEL)
    for i in range(g_mix.shape[0]):
        xa, xb = _encoder_layer(xa, xb, seqlen, g_mix[i], w_in[i], conv_w[i], conv_b[i], dt_bias[i], a_log[i],
                                d_skip[i], ssd_norm_g[i], q_norm_g[i], k_norm_g[i], rpb[i], w_out[i], g_ffn[i],
                                w_pq[i], sub_keys[i], expert_u[i], expert_v[i])
    return (xa.reshape(x_prompt.shape), xb.reshape(x_sample.shape))
```

```python
import functools

import jax
import jax.numpy as jnp
from jax import lax
from jax.experimental import pallas as pl
from jax.experimental.pallas import tpu as pltpu

F32 = jnp.float32
BF16 = jnp.bfloat16

D_MODEL = 1024
GRID_W = 64
EPS = 1e-6
D_INNER = 2048
SSD_HEAD_DIM = 64
SSD_HEADS = 32
SSD_GROUPS = 4
SSD_STATE = 128
CONV_W = 5
CHUNK = 128
CONV_DIM = D_INNER + 2 * SSD_GROUPS * SSD_STATE
NA_HEADS = 16
NA_HEAD_DIM = 64
NA_WIDTH = 1024
WIN_ROWS = 8
WIN_COLS = 16
PEER_HEADS = 8
PEER_N_KEYS = 128
PEER_TOPK = 16
PEER_QUERY_DIM = 256
OFF_XBC = D_INNER
OFF_DT = OFF_XBC + CONV_DIM
OFF_QKV = OFF_DT + 2 * SSD_HEADS
OFF_GATE = OFF_QKV + 3 * NA_WIDTH

LANES = 128
MXU_COLS = 256
COL_QKV = CONV_DIM
COL_Z = COL_QKV + 3 * NA_WIDTH
COL_GATE = COL_Z + D_INNER
COL_DT = COL_GATE + 2 * D_MODEL
INPROJ_TN = 6 * MXU_COLS
PROJ_COLS = -(-(COL_DT + LANES) // INPROJ_TN) * INPROJ_TN
DT_LOCAL = COL_DT - (PROJ_COLS - INPROJ_TN)

VMEM_LIMIT = 56 * 1024 * 1024


def _cparams(sem, flags=None):
    return pltpu.CompilerParams(dimension_semantics=sem, vmem_limit_bytes=VMEM_LIMIT, flags=flags)


def _rms(x, g):
    return x * lax.rsqrt(jnp.mean(x * x, axis=-1, keepdims=True) + EPS) * g


INPROJ_TM = 1024


def _cat_specs(tm, width, na, grid_rank):
    if grid_rank == 1:
        return [pl.BlockSpec((tm, width), lambda i: (jnp.minimum(i, na - 1), 0)),
                pl.BlockSpec((tm, width), lambda i: (jnp.maximum(i - na, 0), 0))]
    return [pl.BlockSpec((tm, width), lambda i, j: (jnp.minimum(i, na - 1), 0)),
            pl.BlockSpec((tm, width), lambda i, j: (jnp.maximum(i - na, 0), 0))]


def _cat_tile(na, a_ref, b_ref):
    return jnp.where(pl.program_id(0) < na, a_ref[...], b_ref[...])


def _inproj_kernel(na, xa_ref, xb_ref, g_ref, w_ref, o_ref, dt_ref, h_scr):
    j = pl.program_id(1)

    @pl.when(j == 0)
    def _():
        h_scr[...] = _rms(_cat_tile(na, xa_ref, xb_ref), g_ref[...]).astype(BF16)

    acc = jnp.dot(h_scr[...], w_ref[...], preferred_element_type=F32)
    o_ref[...] = acc.astype(BF16)

    @pl.when(j == pl.num_programs(1) - 1)
    def _():
        dt_ref[...] = acc[:, DT_LOCAL:DT_LOCAL + LANES]


def _inproj(xa, xb, g_mix, w_cat):
    t = xa.shape[0] + xb.shape[0]
    tm, tn = INPROJ_TM, INPROJ_TN
    assert xa.shape[0] % tm == 0 and xb.shape[0] % tm == 0 and PROJ_COLS % tn == 0 and COL_DT >= PROJ_COLS - tn
    na = xa.shape[0] // tm
    return pl.pallas_call(
        functools.partial(_inproj_kernel, na),
        grid=(t // tm, PROJ_COLS // tn),
        in_specs=_cat_specs(tm, D_MODEL, na, 2) + [
            pl.BlockSpec((1, D_MODEL), lambda i, j: (0, 0)),
            pl.BlockSpec((D_MODEL, tn), lambda i, j: (0, j)),
        ],
        out_specs=[
            pl.BlockSpec((tm, tn), lambda i, j: (i, j)),
            pl.BlockSpec((tm, LANES), lambda i, j: (i, 0)),
        ],
        out_shape=[
            jax.ShapeDtypeStruct((t, PROJ_COLS), BF16),
            jax.ShapeDtypeStruct((t, LANES), F32),
        ],
        scratch_shapes=[pltpu.VMEM((tm, D_MODEL), BF16)],
        compiler_params=_cparams(("parallel", "arbitrary")),
        name="inproj",
    )(xa, xb, g_mix.reshape(1, D_MODEL), w_cat)


CONV_TL = 512
CONV_CB = 1024
HALO = 16
PAD = 8


def _conv_kernel(cur_ref, prev_ref, next_ref, w_ref, b_ref, o_ref, ext):
    i = pl.program_id(1)
    tl = cur_ref.shape[0]
    prev = prev_ref[...].astype(F32)[HALO - PAD:, :]
    nxt = next_ref[...].astype(F32)[:PAD, :]
    ext[0:PAD, :] = jnp.where(i > 0, prev, 0.0)
    ext[PAD:PAD + tl, :] = cur_ref[...].astype(F32)
    ext[PAD + tl:, :] = jnp.where(i < pl.num_programs(1) - 1, nxt, 0.0)
    acc = jnp.broadcast_to(b_ref[...], o_ref.shape)
    for k in range(CONV_W):
        acc = acc + w_ref[k:k + 1, :] * ext[pl.ds(PAD - CONV_W // 2 + k, tl), :]
    o_ref[...] = (acc * jax.nn.sigmoid(acc)).astype(BF16)


def _conv(proj, conv_w, conv_b, nseq, seqlen):
    t = proj.shape[0]
    tl, cb = CONV_TL, CONV_CB
    nl = seqlen // tl
    assert seqlen % tl == 0 and CONV_DIM % cb == 0
    return pl.pallas_call(
        _conv_kernel,
        grid=(nseq, nl, CONV_DIM // cb),
        in_specs=[
            pl.BlockSpec((tl, cb), lambda s, i, c: (s * nl + i, c)),
            pl.BlockSpec((HALO, cb), lambda s, i, c: (jnp.maximum((s * nl + i) * (tl // HALO) - 1, 0), c)),
            pl.BlockSpec((HALO, cb),
                         lambda s, i, c: (jnp.minimum((s * nl + i + 1) * (tl // HALO), t // HALO - 1), c)),
            pl.BlockSpec((CONV_W, cb), lambda s, i, c: (0, c)),
            pl.BlockSpec((1, cb), lambda s, i, c: (0, c)),
        ],
        out_specs=pl.BlockSpec((tl, cb), lambda s, i, c: (s * nl + i, c)),
        out_shape=jax.ShapeDtypeStruct((t, CONV_DIM), BF16),
        scratch_shapes=[pltpu.VMEM((tl + 2 * PAD, cb), F32)],
        compiler_params=_cparams(("parallel", "parallel", "parallel")),
        name="conv",
    )(proj, proj, proj, conv_w, conv_b.reshape(1, CONV_DIM))


def _cumsum_rows(x):
    row = lax.broadcasted_iota(jnp.int32, x.shape, 0)
    d = 1
    while d < x.shape[0]:
        x = x + jnp.where(row >= d, pltpu.roll(x, d, 0), 0.0)
        d *= 2
    return x


def _ssd_direction(reverse, xs_ref, b_ref, c_ref, dt_ref, dtb_ref, alog_ref, y_ref, st):
    row = lax.broadcasted_iota(jnp.int32, (CHUNK, CHUNK), 0)
    col = lax.broadcasted_iota(jnp.int32, (CHUNK, CHUNK), 1)
    lane1 = lax.broadcasted_iota(jnp.int32, (1, LANES), 1)
    dt_pre = dt_ref[...] + dtb_ref[...]
    dt = jnp.maximum(dt_pre, 0.0) + jnp.log1p(jnp.exp(-jnp.abs(dt_pre)))
    dta = dt * (-jnp.exp(alog_ref[...]))
    cs = _cumsum_rows(dta)
    total = cs[CHUNK - 1:CHUNK, :]
    if reverse:
        cum = total - cs + dta
        wexp = cs - dta
        tri = col >= row
    else:
        cum = cs
        wexp = total - cs
        tri = row >= col
    w = jnp.exp(wexp) * dt
    cum_t = cum.T
    dt_t = dt.T
    w_t = w.T
    dec = jnp.exp(total)
    base = SSD_HEADS if reverse else 0
    pairs_per_group = SSD_HEADS // SSD_GROUPS // 2
    for g in range(SSD_GROUPS):
        bg = b_ref[:, g * SSD_STATE:(g + 1) * SSD_STATE]
        cg = c_ref[:, g * SSD_STATE:(g + 1) * SSD_STATE]
        gmat = lax.dot_general(cg, bg, (((1,), (1,)), ((), ())), preferred_element_type=F32)
        bg_t = bg.astype(F32).T
        cg32 = cg.astype(F32)
        for hp in range(pairs_per_group):
            j = g * pairs_per_group + hp
            sl = slice(j * LANES, (j + 1) * LANES)
            xs_pair = xs_ref[:, sl].astype(F32)
            st_pair = st[:, sl]
            y_acc = jnp.zeros((CHUNK, LANES), F32)
            s_acc = jnp.zeros((SSD_STATE, LANES), F32)
            dec_pair = jnp.zeros((1, LANES), F32)
            for e in range(2):
                ln = base + 2 * j + e
                in_head = (col // SSD_HEAD_DIM) == e
                colb = jnp.broadcast_to(cum[:, ln:ln + 1], (CHUNK, CHUNK))
                decay = jnp.where(tri, jnp.exp(colb - cum_t[ln:ln + 1, :]), 0.0)
                m = gmat * decay * dt_t[ln:ln + 1, :]
                ce = cg32 * jnp.exp(colb)
                lhs = jnp.concatenate([m, ce], axis=1).astype(BF16)
                xs_m = jnp.where(in_head, xs_pair, 0.0).astype(BF16)
                st_m = jnp.where(in_head, st_pair, 0.0).astype(BF16)
                rhs = jnp.concatenate([xs_m, st_m], axis=0)
                y_acc = y_acc + jnp.dot(lhs, rhs, preferred_element_type=F32)
                btw = (bg_t * w_t[ln:ln + 1, :]).astype(BF16)
                s_acc = s_acc + jnp.dot(btw, xs_m, preferred_element_type=F32)
                dec_pair = jnp.where((lane1 // SSD_HEAD_DIM) == e,
                                     jnp.broadcast_to(dec[:, ln:ln + 1], (1, LANES)), dec_pair)
            y_ref[:, sl] = y_acc.astype(BF16)
            st[:, sl] = st_pair * dec_pair + s_acc


def _ssd_kernel(xs_f, b_f, c_f, dt_f, xs_b, b_b, c_b, dt_b, dtb_ref, alog_ref, yf_ref, yb_ref, st_f, st_b):
    @pl.when(pl.program_id(1) == 0)
    def _():
        st_f[...] = jnp.zeros_like(st_f)
        st_b[...] = jnp.zeros_like(st_b)

    _ssd_direction(False, xs_f, b_f, c_f, dt_f, dtb_ref, alog_ref, yf_ref, st_f)
    _ssd_direction(True, xs_b, b_b, c_b, dt_b, dtb_ref, alog_ref, yb_ref, st_b)


def _ssd(xbc, dt_raw, dt_bias, a_log, nseq, seqlen):
    t = xbc.shape[0]
    nc = seqlen // CHUNK
    bn = SSD_GROUPS * SSD_STATE
    fwd = lambda s, c: s * nc + c
    bwd = lambda s, c: s * nc + (nc - 1 - c)

    def specs(rb):
        return [
            pl.BlockSpec((CHUNK, D_INNER), lambda s, c: (rb(s, c), 0)),
            pl.BlockSpec((CHUNK, bn), lambda s, c: (rb(s, c), D_INNER // bn)),
            pl.BlockSpec((CHUNK, bn), lambda s, c: (rb(s, c), D_INNER // bn + 1)),
        ]

    dt_spec = lambda rb: pl.BlockSpec((CHUNK, LANES), lambda s, c: (rb(s, c), 0))
    row_spec = pl.BlockSpec((1, LANES), lambda s, c: (0, 0))
    pad = jnp.zeros((LANES - 2 * SSD_HEADS,), F32)
    dtb = jnp.concatenate([dt_bias.reshape(-1), pad]).reshape(1, LANES)
    alog = jnp.concatenate([a_log.reshape(-1), pad]).reshape(1, LANES)
    return pl.pallas_call(
        _ssd_kernel,
        grid=(nseq, nc),
        in_specs=specs(fwd) + [dt_spec(fwd)] + specs(bwd) + [dt_spec(bwd)] + [row_spec, row_spec],
        out_specs=[
            pl.BlockSpec((CHUNK, D_INNER), lambda s, c: (fwd(s, c), 0)),
            pl.BlockSpec((CHUNK, D_INNER), lambda s, c: (bwd(s, c), 0)),
        ],
        out_shape=[jax.ShapeDtypeStruct((t, D_INNER), BF16)] * 2,
        scratch_shapes=[pltpu.VMEM((SSD_STATE, D_INNER), F32)] * 2,
        compiler_params=_cparams(("parallel", "arbitrary")),
        name="ssd",
    )(xbc, xbc, xbc, dt_raw, xbc, xbc, xbc, dt_raw, dtb, alog)


QKN_TM = 512


def _qknorm_kernel(q_ref, k_ref, gq_ref, gk_ref, qo_ref, ko_ref):
    r = lax.broadcasted_iota(jnp.int32, (LANES, LANES), 0) // NA_HEAD_DIM
    c = lax.broadcasted_iota(jnp.int32, (LANES, LANES), 1) // NA_HEAD_DIM
    same_head = jnp.where(r == c, 1.0, 0.0).astype(BF16)
    for src, g_ref, dst, scale in ((q_ref, gq_ref, qo_ref, NA_HEAD_DIM ** -0.5), (k_ref, gk_ref, ko_ref, None)):
        for p in range(NA_WIDTH // LANES):
            sl = slice(p * LANES, (p + 1) * LANES)
            x = src[:, sl].astype(F32)
            sq = x * x
            hi = sq.astype(BF16)
            lo = (sq - hi.astype(F32)).astype(BF16)
            ss = (jnp.dot(hi, same_head, preferred_element_type=F32)
                  + jnp.dot(lo, same_head, preferred_element_type=F32))
            y = x * lax.rsqrt(ss / NA_HEAD_DIM + EPS) * g_ref[...]
            if scale is not None:
                y = y * scale
            dst[:, sl] = y.astype(BF16)


def _qknorm(proj, q_norm_g, k_norm_g):
    t = proj.shape[0]
    tm = QKN_TM
    reps = LANES // NA_HEAD_DIM
    gq = jnp.tile(q_norm_g, reps).reshape(1, LANES)
    gk = jnp.tile(k_norm_g, reps).reshape(1, LANES)
    qcol = COL_QKV // NA_WIDTH
    return pl.pallas_call(
        _qknorm_kernel,
        grid=(t // tm,),
        in_specs=[
            pl.BlockSpec((tm, NA_WIDTH), lambda i: (i, qcol)),
            pl.BlockSpec((tm, NA_WIDTH), lambda i: (i, qcol + 1)),
            pl.BlockSpec((1, LANES), lambda i: (0, 0)),
            pl.BlockSpec((1, LANES), lambda i: (0, 0)),
        ],
        out_specs=[pl.BlockSpec((tm, NA_WIDTH), lambda i: (i, 0))] * 2,
        out_shape=[jax.ShapeDtypeStruct((t, NA_WIDTH), BF16)] * 2,
        compiler_params=_cparams(("parallel",)),
        name="qknorm",
    )(proj, proj, gq, gk)


def _natten_kernel(*refs):
    q_ref = refs[0]
    k_refs = refs[1:1 + WIN_ROWS]
    v_refs = refs[1 + WIN_ROWS:1 + 2 * WIN_ROWS]
    bias_ref = refs[1 + 2 * WIN_ROWS]
    o_ref = refs[2 + 2 * WIN_ROWS]
    first = lax.broadcasted_iota(jnp.int32, (GRID_W, LANES), 1) < NA_HEAD_DIM
    for p in range(NA_WIDTH // LANES):
        sl = slice(p * LANES, (p + 1) * LANES)
        qp = q_ref[:, sl].astype(F32)
        qrows = jnp.concatenate([jnp.where(first, qp, 0.0), jnp.where(first, 0.0, qp)], axis=0).astype(BF16)
        kp = jnp.concatenate([r[:, sl] for r in k_refs], axis=0)
        vp = jnp.concatenate([r[:, sl] for r in v_refs], axis=0)
        st = lax.dot_general(kp, qrows, (((1,), (1,)), ((), ())), preferred_element_type=F32)
        st = st + bias_ref[0, p]
        m = jnp.max(st, axis=0, keepdims=True)
        pe = jnp.exp(st - m)
        inv = 1.0 / jnp.sum(pe, axis=0, keepdims=True)
        pn = (pe * inv).astype(BF16)
        o2 = lax.dot_general(pn, vp, (((0,), (0,)), ((), ())), preferred_element_type=F32)
        o_ref[:, sl] = jnp.where(first, o2[:GRID_W], o2[GRID_W:]).astype(BF16)


def _na_bias_table(rpb):
    cols = jnp.arange(GRID_W)
    c0 = jnp.clip(cols - WIN_COLS // 2, 0, GRID_W - WIN_COLS)
    j = jnp.arange(GRID_W)
    valid = (j[None, :] >= c0[:, None]) & (j[None, :] < c0[:, None] + WIN_COLS)
    k = jnp.arange(2 * WIN_COLS - 1)
    hp = LANES // NA_HEAD_DIM
    x = jnp.arange(LANES)
    onehot = ((x[None, None, None, :] // GRID_W == jnp.arange(hp)[:, None, None, None])
              & (k[None, :, None, None] == j[None, None, :, None] - x[None, None, None, :] % GRID_W + (WIN_COLS - 1))
              ).astype(F32)
    toep = jnp.einsum("pgrk,gkjx->prjx", rpb.astype(F32).reshape(NA_HEADS // hp, hp, *rpb.shape[1:]), onehot,
                      precision=lax.Precision.HIGHEST)
    valid_x = jnp.tile(valid.T, (1, hp))
    toep = jnp.where(valid_x[None, None], toep, -jnp.inf)
    per_delta = jnp.stack([toep[:, WIN_ROWS - 1 - d:2 * WIN_ROWS - 1 - d] for d in range(WIN_ROWS)], axis=0)
    return per_delta.reshape(WIN_ROWS, NA_HEADS // hp, WIN_ROWS * GRID_W, LANES)


def _natten(qn, kn, proj, rpb, nseq, seqlen):
    t = qn.shape[0]
    rows = seqlen // GRID_W
    assert rows >= WIN_ROWS
    r0 = lambda r: jnp.clip(r - WIN_ROWS // 2, 0, rows - WIN_ROWS)
    vcol = COL_QKV // NA_WIDTH + 2
    tbl = _na_bias_table(rpb)

    def win_spec(i, colblk):
        return pl.BlockSpec((GRID_W, NA_WIDTH), lambda s, r: (s * rows + r0(r) + i, colblk))

    return pl.pallas_call(
        _natten_kernel,
        grid=(nseq, rows),
        in_specs=([pl.BlockSpec((GRID_W, NA_WIDTH), lambda s, r: (s * rows + r, 0))]
                  + [win_spec(i, 0) for i in range(WIN_ROWS)]
                  + [win_spec(i, vcol) for i in range(WIN_ROWS)]
                  + [pl.BlockSpec((1,) + tbl.shape[1:], lambda s, r: (r - r0(r), 0, 0, 0))]),
        out_specs=pl.BlockSpec((GRID_W, NA_WIDTH), lambda s, r: (s * rows + r, 0)),
        out_shape=jax.ShapeDtypeStruct((t, NA_WIDTH), BF16),
        compiler_params=_cparams(("parallel", "arbitrary")),
        name="natten",
    )(qn, *([kn] * WIN_ROWS), *([proj] * WIN_ROWS), tbl)


OUTPROJ_TM = 512


def _outproj_kernel(na, yf_ref, yb_ref, xs_ref, z_ref, g1_ref, g2_ref, yna_ref, xa_ref, xb_ref, w1_ref, w2_ref,
                    dsk_ref, ng_ref, gffn_ref, x1_ref, t_ref):
    y = yf_ref[...].astype(F32) + yb_ref[...].astype(F32) + dsk_ref[...] * xs_ref[...].astype(F32)
    y = y * jax.nn.silu(z_ref[...].astype(F32))
    yn = _rms(y, ng_ref[...]).astype(BF16)
    br_ssd = jnp.dot(yn, w1_ref[...], preferred_element_type=F32)
    br_na = jnp.dot(yna_ref[...], w2_ref[...], preferred_element_type=F32)
    mix = (jax.nn.sigmoid(g1_ref[...].astype(F32)) * br_ssd
           + jax.nn.sigmoid(g2_ref[...].astype(F32)) * br_na)
    x1 = _cat_tile(na, xa_ref, xb_ref) + mix
    x1_ref[...] = x1
    t_ref[...] = pltpu.bitcast(_rms(x1, gffn_ref[...]).astype(BF16), jnp.uint32)


def _outproj(yf, yb, xbc, proj, yna, xa, xb, w_out, d_skip, ssd_norm_g, g_ffn):
    t = xa.shape[0] + xb.shape[0]
    tm = OUTPROJ_TM
    na = xa.shape[0] // tm
    tok = lambda w, cb: pl.BlockSpec((tm, w), lambda i: (i, cb))
    row = lambda w: pl.BlockSpec((1, w), lambda i: (0, 0))
    dsk = jnp.repeat(d_skip, SSD_HEAD_DIM).reshape(1, D_INNER)
    return pl.pallas_call(
        functools.partial(_outproj_kernel, na),
        grid=(t // tm,),
        in_specs=[
            tok(D_INNER, 0), tok(D_INNER, 0), tok(D_INNER, 0),
            tok(D_INNER, COL_Z // D_INNER),
            tok(D_MODEL, COL_GATE // D_MODEL), tok(D_MODEL, COL_GATE // D_MODEL + 1),
            tok(NA_WIDTH, 0), *_cat_specs(tm, D_MODEL, na, 1),
            pl.BlockSpec((D_INNER, D_MODEL), lambda i: (0, 0)),
            pl.BlockSpec((NA_WIDTH, D_MODEL), lambda i: (D_INNER // NA_WIDTH, 0)),
            row(D_INNER), row(D_INNER), row(D_MODEL),
        ],
        out_specs=[tok(D_MODEL, 0), pl.BlockSpec((tm // 2, D_MODEL), lambda i: (i, 0))],
        out_shape=[jax.ShapeDtypeStruct((t, D_MODEL), F32), jax.ShapeDtypeStruct((t // 2, D_MODEL), jnp.uint32)],
        compiler_params=_cparams(("parallel",)),
        name="outproj",
    )(yf, yb, xbc, proj, proj, proj, yna, xa, xb, w_out, w_out, dsk,
      ssd_norm_g.reshape(1, D_INNER), g_ffn.reshape(1, D_MODEL))


ROUTE_TR = 256
NEG_INF = float("-inf")
RANK_NONE = 64.0


def _top_values(ss, k, out_refs=None, want_rank=None):
    n = len(ss)
    ss = list(ss)
    out_refs = out_refs or [None] * n
    want_rank = want_rank or [False] * n
    ranks = [jnp.full(s.shape, RANK_NONE, F32) if w else None for s, w in zip(ss, want_rank)]
    mx = [None] * n
    for r in range(k):
        for i in range(n):
            mx[i] = jnp.max(ss[i], axis=0, keepdims=True)
            if out_refs[i] is not None:
                out_refs[i][r:r + 1, :] = mx[i]
            hit = ss[i] == mx[i]
            if ranks[i] is not None:
                ranks[i] = jnp.where(hit, float(r), ranks[i])
            ss[i] = jnp.where(hit, NEG_INF, ss[i])
    return mx, ranks


ROUTE_HEAD_GROUP = 4


def _route_kernel(t_ref, wpq_ref, sk_ref, rank2_ref, b_ref, cnt_ref, a_ref, top1, top2):
    half = PEER_QUERY_DIM // 2
    sub = 8
    tokens = pltpu.bitcast(t_ref[...], BF16)
    row = lax.broadcasted_iota(jnp.int32, (sub, tokens.shape[0]), 0)
    q = jnp.dot(tokens, wpq_ref[...], preferred_element_type=F32).astype(BF16)
    for h0 in range(0, PEER_HEADS, ROUTE_HEAD_GROUP):
        heads = range(h0, h0 + ROUTE_HEAD_GROUP)
        s1, s2 = [], []
        for h in heads:
            for side, dst in ((0, s1), (1, s2)):
                qb = q[:, (2 * h + side) * half:(2 * h + side + 1) * half]
                dst.append(lax.dot_general(sk_ref[side], qb, (((1,), (1,)), ((), ())),
                                           preferred_element_type=F32))
        _, ranks = _top_values(s1 + s2, PEER_TOPK, [top1.at[h] for h in heads] + [top2.at[h] for h in heads],
                               [False] * len(s1) + [True] * len(s2))
        rank2 = ranks[len(s1):]
        cands = []
        for h in heads:
            t1, t2 = top1.at[h], top2.at[h]
            tiles = [t1[0:1, :] + t2[0:sub, :], t1[0:1, :] + t2[sub:, :]]
            for r1 in range(1, sub):
                tile = t1[r1:r1 + 1, :] + t2[0:sub, :]
                tiles.append(jnp.where(row < PEER_TOPK // (r1 + 1), tile, NEG_INF))
            tiles.append(t1[sub:, :] + t2[0:1, :])
            cands.append(jnp.concatenate(tiles, axis=0))
        taus, _ = _top_values(cands, PEER_TOPK)
        for g, h in enumerate(heads):
            t1, t2 = top1.at[h], top2.at[h]
            cand, tau = cands[g], taus[g]
            m1 = t1[0:1, :]
            m2 = t2[0:1, :]
            z = jnp.sum(jnp.where(cand >= tau, jnp.exp(cand - (m1 + m2)), 0.0), axis=0, keepdims=True)
            first = t1[...]
            cnt = jnp.zeros_like(s1[g])
            for r2 in range(PEER_TOPK):
                passing = jnp.where((first + t2[r2:r2 + 1, :]) >= tau, first, jnp.inf)
                cnt = jnp.where(s1[g] >= jnp.min(passing, axis=0, keepdims=True), float(r2 + 1), cnt)
            rank2_ref[h] = pltpu.bitcast(rank2[g].astype(BF16), jnp.uint32)
            b_ref[h] = pltpu.bitcast((jnp.exp(s2[g] - m2) / z).astype(BF16), jnp.uint32)
            cnt_ref[h] = cnt
            a_ref[h] = jnp.exp(s1[g] - m1)


def _route(tn, w_pq, sub_keys):
    t = 2 * tn.shape[0]
    tr = ROUTE_TR
    qd = PEER_HEADS * PEER_QUERY_DIM
    out_spec = pl.BlockSpec((PEER_HEADS, PEER_N_KEYS, tr), lambda i: (0, 0, i))
    key_shape = (PEER_HEADS, PEER_N_KEYS, t)
    pair_spec = pl.BlockSpec((PEER_HEADS, PEER_N_KEYS // 2, tr), lambda i: (0, 0, i))
    pair_shape = (PEER_HEADS, PEER_N_KEYS // 2, t)
    return pl.pallas_call(
        _route_kernel,
        grid=(t // tr,),
        in_specs=[
            pl.BlockSpec((tr // 2, D_MODEL), lambda i: (i, 0)),
            pl.BlockSpec((D_MODEL, qd), lambda i: (0, 0)),
            pl.BlockSpec((2, PEER_N_KEYS, PEER_QUERY_DIM // 2), lambda i: (0, 0, 0)),
        ],
        out_specs=[pair_spec, pair_spec, out_spec, out_spec],
        out_shape=[jax.ShapeDtypeStruct(pair_shape, jnp.uint32), jax.ShapeDtypeStruct(pair_shape, jnp.uint32),
                   jax.ShapeDtypeStruct(key_shape, F32), jax.ShapeDtypeStruct(key_shape, F32)],
        scratch_shapes=[pltpu.VMEM((PEER_HEADS, PEER_TOPK, tr), F32)] * 2,
        compiler_params=_cparams(("parallel",)),
        name="route",
    )(tn, w_pq, sub_keys)


PEER_TP = 512
PEER_I1 = 8
PEER_EB = PEER_I1 * PEER_N_KEYS
NT_DIMS = (((1,), (1,)), ((), ()))
GATE_ROWS = PEER_N_KEYS


def _gate_block(first_keys, st_slot, g_slot, rank2_ref, b_ref, cnt_ref, a_ref):
    rows = GATE_ROWS
    for i in first_keys:
        for part in range(PEER_N_KEYS // rows):
            r0 = part * rows
            e0 = i * PEER_N_KEYS + r0
            for ck in range(st_slot.shape[1] // LANES):
                cols = slice(ck * LANES, (ck + 1) * LANES)
                wgt = jnp.zeros((rows, LANES), BF16)
                for h in range(PEER_HEADS):
                    cnt = cnt_ref[h, i:i + 1, cols].astype(BF16)
                    a1 = a_ref[h, i:i + 1, cols].astype(BF16)
                    rank2 = pltpu.bitcast(rank2_ref[h, r0 // 2:(r0 + rows) // 2, cols], BF16)
                    b2 = pltpu.bitcast(b_ref[h, r0 // 2:(r0 + rows) // 2, cols], BF16)
                    wgt = wgt + jnp.where(rank2 < cnt, b2, jnp.zeros_like(b2)) * a1
                pre = st_slot[e0:e0 + rows, cols].astype(BF16)
                act = 0.5 * pre * (1.0 + lax.erf(pre * (2.0 ** -0.5)))
                g_slot[e0 // 2:(e0 + rows) // 2, cols] = pltpu.bitcast(act * wgt, jnp.uint32)


def _peer_kernel(na, t_ref, u_ref, vt_ref, rank2_ref, b_ref, cnt_lo, a_lo, cnt_hi, a_hi, x1_ref, oa_ref, ob_ref,
                 acc, st0, st1, gated0, gated1):
    jj = pl.program_id(1)
    last = pl.num_programs(1) - 1
    eb = PEER_EB
    st = (st0, st1)
    gated = (gated0, gated1)

    def accumulate(k):
        acc[...] = jnp.dot(pltpu.bitcast(vt_ref[:, k * eb:(k + 1) * eb], BF16),
                           pltpu.bitcast(gated[k][...], BF16), preferred_element_type=F32) + acc[...]

    def gate(k):
        cnt_ref, a_ref = (cnt_lo, a_lo) if k == 0 else (cnt_hi, a_hi)
        _gate_block(range(PEER_I1), st[1 - k], gated[1 - k], rank2_ref, b_ref, cnt_ref, a_ref)

    def scores(k):
        st[k][...] = lax.dot_general(pltpu.bitcast(u_ref[k * eb // 2:(k + 1) * eb // 2, :], BF16),
                                     pltpu.bitcast(t_ref[...], BF16), NT_DIMS, preferred_element_type=F32)

    @pl.when(jj == 0)
    def _():
        acc[...] = jnp.zeros_like(acc)
        scores(0)
        gate(1)
        scores(1)

    @pl.when((jj > 0) & (jj < last))
    def _():
        for k in range(2):
            scores(k)
            gate(k)
            accumulate(k)

    @pl.when(jj == last)
    def _():
        accumulate(0)
        gate(0)
        accumulate(1)

    for o_ref, mine in ((oa_ref, pl.program_id(0) < na), (ob_ref, pl.program_id(0) >= na)):
        @pl.when((jj == last) & mine)
        def _(o_ref=o_ref):
            o_ref[...] = x1_ref[...] + acc[...].T


def _peer(tn, u_bf, vt_bf, rank2, b, cnt, a, x1, ta):
    t = x1.shape[0]
    tp, eb = PEER_TP, PEER_EB
    assert ta % tp == 0
    na = ta // tp
    nb = 2 * u_bf.shape[0] // eb
    assert nb % 2 == 0
    pair_spec = pl.BlockSpec((PEER_HEADS, PEER_N_KEYS // 2, tp), lambda i, j: (0, 0, i))
    lo_spec = pl.BlockSpec((PEER_HEADS, PEER_I1, tp), lambda i, j: (0, jnp.maximum(2 * j - 1, 0), i))
    hi_spec = pl.BlockSpec((PEER_HEADS, PEER_I1, tp), lambda i, j: (0, jnp.minimum(2 * j, nb - 1), i))
    return pl.pallas_call(
        functools.partial(_peer_kernel, na),
        grid=(t // tp, nb // 2 + 1),
        in_specs=[
            pl.BlockSpec((tp // 2, D_MODEL), lambda i, j: (i, 0)),
            pl.BlockSpec((eb, D_MODEL), lambda i, j: (jnp.minimum(j, nb // 2 - 1), 0)),
            pl.BlockSpec((D_MODEL // 2, 2 * eb), lambda i, j: (0, jnp.maximum(j - 1, 0))),
            pair_spec, pair_spec, lo_spec, lo_spec, hi_spec, hi_spec,
            pl.BlockSpec((tp, D_MODEL), lambda i, j: (i, 0)),
        ],
        out_specs=_cat_specs(tp, D_MODEL, na, 2),
        out_shape=[jax.ShapeDtypeStruct((ta, D_MODEL), F32), jax.ShapeDtypeStruct((t - ta, D_MODEL), F32)],
        scratch_shapes=[pltpu.VMEM((D_MODEL, tp), F32), pltpu.VMEM((eb, tp), F32), pltpu.VMEM((eb, tp), F32),
                        pltpu.VMEM((eb // 2, tp), jnp.uint32), pltpu.VMEM((eb // 2, tp), jnp.uint32)],
        compiler_params=_cparams(("arbitrary", "arbitrary")),
        name="peer",
    )(tn, u_bf, vt_bf, rank2, b, cnt, a, cnt, a, x1)


PACK_TILE = 1024


def _pack_kernel(transpose, w_ref, o_ref):
    w = w_ref[...]
    if transpose:
        w = w.T
    o_ref[...] = pltpu.bitcast(w.astype(BF16), jnp.uint32)


def _pack_table(w, transpose):
    n, c = w.shape
    tile = PACK_TILE
    assert n % tile == 0 and c == tile
    if transpose:
        out_shape, out_spec = (c // 2, n), pl.BlockSpec((c // 2, tile), lambda i: (0, i))
    else:
        out_shape, out_spec = (n // 2, c), pl.BlockSpec((tile // 2, c), lambda i: (i, 0))
    return pl.pallas_call(
        functools.partial(_pack_kernel, transpose),
        grid=(n // tile,),
        in_specs=[pl.BlockSpec((tile, c), lambda i: (i, 0))],
        out_specs=out_spec,
        out_shape=jax.ShapeDtypeStruct(out_shape, jnp.uint32),
        compiler_params=_cparams(("parallel",)),
        name="pack_t" if transpose else "pack",
    )(w)


def _regroup_w_in(w_in):
    pad = jnp.zeros((D_MODEL, PROJ_COLS - COL_DT - 2 * SSD_HEADS), w_in.dtype)
    return jnp.concatenate([w_in[:, OFF_XBC:OFF_DT], w_in[:, OFF_QKV:OFF_GATE], w_in[:, :OFF_XBC],
                            w_in[:, OFF_GATE:], w_in[:, OFF_DT:OFF_QKV], pad], axis=1).astype(BF16)


def _encoder_layer(xa, xb, seqlen, g_mix, w_in, conv_w, conv_b, dt_bias, a_log, d_skip, ssd_norm_g,
                   q_norm_g, k_norm_g, rpb, w_out, g_ffn, w_pq, sub_keys, expert_u, expert_v):
    nseq = (xa.shape[0] + xb.shape[0]) // seqlen
    proj, dt_raw = _inproj(xa, xb, g_mix, _regroup_w_in(w_in))
    xbc = _conv(proj, conv_w, conv_b, nseq, seqlen)
    yf, yb = _ssd(xbc, dt_raw, dt_bias, a_log, nseq, seqlen)
    qn, kn = _qknorm(proj, q_norm_g, k_norm_g)
    yna = _natten(qn, kn, proj, rpb, nseq, seqlen)
    x1, tn = _outproj(yf, yb, xbc, proj, yna, xa, xb, w_out.astype(BF16), d_skip, ssd_norm_g, g_ffn)
    rank2, b, cnt, a = _route(tn, w_pq.astype(BF16), sub_keys.astype(BF16))
    return _peer(tn, _pack_table(expert_u, False), _pack_table(expert_v, True), rank2, b, cnt, a, x1,
                 xa.shape[0])


def kernel(x_prompt, x_sample, g_mix, w_in, conv_w, conv_b, dt_bias, a_log, d_skip, ssd_norm_g, q_norm_g,
           k_norm_g, rpb, w_out, g_ffn, w_pq, sub_keys, expert_u, expert_v):
    assert x_prompt.shape[1:] == x_sample.shape[1:]
    seqlen = x_prompt.shape[1]
    xa = x_prompt.reshape(-1, D_MODEL)
    xb = x_sample.reshape(-1, D_MODEL)
    for i in range(g_mix.shape[0]):
        xa, xb = _encoder_layer(xa, xb, seqlen, g_mix[i], w_in[i], conv_w[i], conv_b[i], dt_bias[i], a_log[i],
                                d_skip[i], ssd_norm_g[i], q_norm_g[i], k_norm_g[i], rpb[i], w_out[i], g_ffn[i],
                                w_pq[i], sub_keys[i], expert_u[i], expert_v[i])
    return (xa.reshape(x_prompt.shape), xb.reshape(x_sample.shape))
```

```python
import functools

import jax
import jax.numpy as jnp
from jax import lax
from jax.experimental import pallas as pl
from jax.experimental.pallas import tpu as pltpu

F32 = jnp.float32
BF16 = jnp.bfloat16

D_MODEL = 1024
GRID_W = 64
EPS = 1e-6
D_INNER = 2048
SSD_HEAD_DIM = 64
SSD_HEADS = 32
SSD_GROUPS = 4
SSD_STATE = 128
CONV_W = 5
CHUNK = 128
CONV_DIM = D_INNER + 2 * SSD_GROUPS * SSD_STATE
NA_HEADS = 16
NA_HEAD_DIM = 64
NA_WIDTH = 1024
WIN_ROWS = 8
WIN_COLS = 16
PEER_HEADS = 8
PEER_N_KEYS = 128
PEER_TOPK = 16
PEER_QUERY_DIM = 256
OFF_XBC = D_INNER
OFF_DT = OFF_XBC + CONV_DIM
OFF_QKV = OFF_DT + 2 * SSD_HEADS
OFF_GATE = OFF_QKV + 3 * NA_WIDTH

LANES = 128
MXU_COLS = 256
COL_QKV = CONV_DIM
COL_Z = COL_QKV + 3 * NA_WIDTH
COL_GATE = COL_Z + D_INNER
COL_DT = COL_GATE + 2 * D_MODEL
INPROJ_TN = 6 * MXU_COLS
PROJ_COLS = -(-(COL_DT + LANES) // INPROJ_TN) * INPROJ_TN
DT_LOCAL = COL_DT - (PROJ_COLS - INPROJ_TN)

VMEM_LIMIT = 56 * 1024 * 1024


def _cparams(sem, flags=None):
    return pltpu.CompilerParams(dimension_semantics=sem, vmem_limit_bytes=VMEM_LIMIT, flags=flags)


def _rms(x, g):
    return x * lax.rsqrt(jnp.mean(x * x, axis=-1, keepdims=True) + EPS) * g


INPROJ_TM = 1024


def _cat_specs(tm, width, na, grid_rank):
    if grid_rank == 1:
        return [pl.BlockSpec((tm, width), lambda i: (jnp.minimum(i, na - 1), 0)),
                pl.BlockSpec((tm, width), lambda i: (jnp.maximum(i - na, 0), 0))]
    return [pl.BlockSpec((tm, width), lambda i, j: (jnp.minimum(i, na - 1), 0)),
            pl.BlockSpec((tm, width), lambda i, j: (jnp.maximum(i - na, 0), 0))]


def _cat_tile(na, a_ref, b_ref):
    return jnp.where(pl.program_id(0) < na, a_ref[...], b_ref[...])


def _inproj_kernel(na, xa_ref, xb_ref, g_ref, w_ref, o_ref, dt_ref, h_scr):
    j = pl.program_id(1)

    @pl.when(j == 0)
    def _():
        h_scr[...] = _rms(_cat_tile(na, xa_ref, xb_ref), g_ref[...]).astype(BF16)

    acc = jnp.dot(h_scr[...], w_ref[...], preferred_element_type=F32)
    o_ref[...] = acc.astype(BF16)

    @pl.when(j == pl.num_programs(1) - 1)
    def _():
        dt_ref[...] = acc[:, DT_LOCAL:DT_LOCAL + LANES]


def _inproj(xa, xb, g_mix, w_cat):
    t = xa.shape[0] + xb.shape[0]
    tm, tn = INPROJ_TM, INPROJ_TN
    assert xa.shape[0] % tm == 0 and xb.shape[0] % tm == 0 and PROJ_COLS % tn == 0 and COL_DT >= PROJ_COLS - tn
    na = xa.shape[0] // tm
    return pl.pallas_call(
        functools.partial(_inproj_kernel, na),
        grid=(t // tm, PROJ_COLS // tn),
        in_specs=_cat_specs(tm, D_MODEL, na, 2) + [
            pl.BlockSpec((1, D_MODEL), lambda i, j: (0, 0)),
            pl.BlockSpec((D_MODEL, tn), lambda i, j: (0, j)),
        ],
        out_specs=[
            pl.BlockSpec((tm, tn), lambda i, j: (i, j)),
            pl.BlockSpec((tm, LANES), lambda i, j: (i, 0)),
        ],
        out_shape=[
            jax.ShapeDtypeStruct((t, PROJ_COLS), BF16),
            jax.ShapeDtypeStruct((t, LANES), F32),
        ],
        scratch_shapes=[pltpu.VMEM((tm, D_MODEL), BF16)],
        compiler_params=_cparams(("parallel", "arbitrary")),
        name="inproj",
    )(xa, xb, g_mix.reshape(1, D_MODEL), w_cat)


CONV_TL = 512
CONV_CB = 1024
HALO = 16
PAD = 8


def _conv_kernel(cur_ref, prev_ref, next_ref, w_ref, b_ref, o_ref, ext):
    i = pl.program_id(1)
    tl = cur_ref.shape[0]
    prev = prev_ref[...].astype(F32)[HALO - PAD:, :]
    nxt = next_ref[...].astype(F32)[:PAD, :]
    ext[0:PAD, :] = jnp.where(i > 0, prev, 0.0)
    ext[PAD:PAD + tl, :] = cur_ref[...].astype(F32)
    ext[PAD + tl:, :] = jnp.where(i < pl.num_programs(1) - 1, nxt, 0.0)
    acc = jnp.broadcast_to(b_ref[...], o_ref.shape)
    for k in range(CONV_W):
        acc = acc + w_ref[k:k + 1, :] * ext[pl.ds(PAD - CONV_W // 2 + k, tl), :]
    o_ref[...] = (acc * jax.nn.sigmoid(acc)).astype(BF16)


def _conv(proj, conv_w, conv_b, nseq, seqlen):
    t = proj.shape[0]
    tl, cb = CONV_TL, CONV_CB
    nl = seqlen // tl
    assert seqlen % tl == 0 and CONV_DIM % cb == 0
    return pl.pallas_call(
        _conv_kernel,
        grid=(nseq, nl, CONV_DIM // cb),
        in_specs=[
            pl.BlockSpec((tl, cb), lambda s, i, c: (s * nl + i, c)),
            pl.BlockSpec((HALO, cb), lambda s, i, c: (jnp.maximum((s * nl + i) * (tl // HALO) - 1, 0), c)),
            pl.BlockSpec((HALO, cb),
                         lambda s, i, c: (jnp.minimum((s * nl + i + 1) * (tl // HALO), t // HALO - 1), c)),
            pl.BlockSpec((CONV_W, cb), lambda s, i, c: (0, c)),
            pl.BlockSpec((1, cb), lambda s, i, c: (0, c)),
        ],
        out_specs=pl.BlockSpec((tl, cb), lambda s, i, c: (s * nl + i, c)),
        out_shape=jax.ShapeDtypeStruct((t, CONV_DIM), BF16),
        scratch_shapes=[pltpu.VMEM((tl + 2 * PAD, cb), F32)],
        compiler_params=_cparams(("parallel", "parallel", "parallel")),
        name="conv",
    )(proj, proj, proj, conv_w, conv_b.reshape(1, CONV_DIM))


def _cumsum_rows(x):
    row = lax.broadcasted_iota(jnp.int32, x.shape, 0)
    d = 1
    while d < x.shape[0]:
        x = x + jnp.where(row >= d, pltpu.roll(x, d, 0), 0.0)
        d *= 2
    return x


def _ssd_direction(reverse, xs_ref, b_ref, c_ref, dt_ref, dtb_ref, alog_ref, y_ref, st):
    row = lax.broadcasted_iota(jnp.int32, (CHUNK, CHUNK), 0)
    col = lax.broadcasted_iota(jnp.int32, (CHUNK, CHUNK), 1)
    lane1 = lax.broadcasted_iota(jnp.int32, (1, LANES), 1)
    dt_pre = dt_ref[...] + dtb_ref[...]
    dt = jnp.maximum(dt_pre, 0.0) + jnp.log1p(jnp.exp(-jnp.abs(dt_pre)))
    dta = dt * (-jnp.exp(alog_ref[...]))
    cs = _cumsum_rows(dta)
    total = cs[CHUNK - 1:CHUNK, :]
    if reverse:
        cum = total - cs + dta
        wexp = cs - dta
        tri = col >= row
    else:
        cum = cs
        wexp = total - cs
        tri = row >= col
    w = jnp.exp(wexp) * dt
    cum_t = cum.T
    dt_t = dt.T
    w_t = w.T
    dec = jnp.exp(total)
    base = SSD_HEADS if reverse else 0
    pairs_per_group = SSD_HEADS // SSD_GROUPS // 2
    for g in range(SSD_GROUPS):
        bg = b_ref[:, g * SSD_STATE:(g + 1) * SSD_STATE]
        cg = c_ref[:, g * SSD_STATE:(g + 1) * SSD_STATE]
        gmat = lax.dot_general(cg, bg, (((1,), (1,)), ((), ())), preferred_element_type=F32)
        bg_t = bg.astype(F32).T
        cg32 = cg.astype(F32)
        for hp in range(pairs_per_group):
            j = g * pairs_per_group + hp
            sl = slice(j * LANES, (j + 1) * LANES)
            xs_pair = xs_ref[:, sl].astype(F32)
            st_pair = st[:, sl]
            y_acc = jnp.zeros((CHUNK, LANES), F32)
            s_acc = jnp.zeros((SSD_STATE, LANES), F32)
            dec_pair = jnp.zeros((1, LANES), F32)
            for e in range(2):
                ln = base + 2 * j + e
                in_head = (col // SSD_HEAD_DIM) == e
                colb = jnp.broadcast_to(cum[:, ln:ln + 1], (CHUNK, CHUNK))
                decay = jnp.where(tri, jnp.exp(colb - cum_t[ln:ln + 1, :]), 0.0)
                m = gmat * decay * dt_t[ln:ln + 1, :]
                ce = cg32 * jnp.exp(colb)
                lhs = jnp.concatenate([m, ce], axis=1).astype(BF16)
                xs_m = jnp.where(in_head, xs_pair, 0.0).astype(BF16)
                st_m = jnp.where(in_head, st_pair, 0.0).astype(BF16)
                rhs = jnp.concatenate([xs_m, st_m], axis=0)
                y_acc = y_acc + jnp.dot(lhs, rhs, preferred_element_type=F32)
                btw = (bg_t * w_t[ln:ln + 1, :]).astype(BF16)
                s_acc = s_acc + jnp.dot(btw, xs_m, preferred_element_type=F32)
                dec_pair = jnp.where((lane1 // SSD_HEAD_DIM) == e,
                                     jnp.broadcast_to(dec[:, ln:ln + 1], (1, LANES)), dec_pair)
            y_ref[:, sl] = y_acc.astype(BF16)
            st[:, sl] = st_pair * dec_pair + s_acc


def _ssd_kernel(xs_f, b_f, c_f, dt_f, xs_b, b_b, c_b, dt_b, dtb_ref, alog_ref, yf_ref, yb_ref, st_f, st_b):
    @pl.when(pl.program_id(1) == 0)
    def _():
        st_f[...] = jnp.zeros_like(st_f)
        st_b[...] = jnp.zeros_like(st_b)

    _ssd_direction(False, xs_f, b_f, c_f, dt_f, dtb_ref, alog_ref, yf_ref, st_f)
    _ssd_direction(True, xs_b, b_b, c_b, dt_b, dtb_ref, alog_ref, yb_ref, st_b)


def _ssd(xbc, dt_raw, dt_bias, a_log, nseq, seqlen):
    t = xbc.shape[0]
    nc = seqlen // CHUNK
    bn = SSD_GROUPS * SSD_STATE
    fwd = lambda s, c: s * nc + c
    bwd = lambda s, c: s * nc + (nc - 1 - c)

    def specs(rb):
        return [
            pl.BlockSpec((CHUNK, D_INNER), lambda s, c: (rb(s, c), 0)),
            pl.BlockSpec((CHUNK, bn), lambda s, c: (rb(s, c), D_INNER // bn)),
            pl.BlockSpec((CHUNK, bn), lambda s, c: (rb(s, c), D_INNER // bn + 1)),
        ]

    dt_spec = lambda rb: pl.BlockSpec((CHUNK, LANES), lambda s, c: (rb(s, c), 0))
    row_spec = pl.BlockSpec((1, LANES), lambda s, c: (0, 0))
    pad = jnp.zeros((LANES - 2 * SSD_HEADS,), F32)
    dtb = jnp.concatenate([dt_bias.reshape(-1), pad]).reshape(1, LANES)
    alog = jnp.concatenate([a_log.reshape(-1), pad]).reshape(1, LANES)
    return pl.pallas_call(
        _ssd_kernel,
        grid=(nseq, nc),
        in_specs=specs(fwd) + [dt_spec(fwd)] + specs(bwd) + [dt_spec(bwd)] + [row_spec, row_spec],
        out_specs=[
            pl.BlockSpec((CHUNK, D_INNER), lambda s, c: (fwd(s, c), 0)),
            pl.BlockSpec((CHUNK, D_INNER), lambda s, c: (bwd(s, c), 0)),
        ],
        out_shape=[jax.ShapeDtypeStruct((t, D_INNER), BF16)] * 2,
        scratch_shapes=[pltpu.VMEM((SSD_STATE, D_INNER), F32)] * 2,
        compiler_params=_cparams(("parallel", "arbitrary")),
        name="ssd",
    )(xbc, xbc, xbc, dt_raw, xbc, xbc, xbc, dt_raw, dtb, alog)


QKN_TM = 512


def _qknorm_kernel(q_ref, k_ref, gq_ref, gk_ref, qo_ref, ko_ref):
    r = lax.broadcasted_iota(jnp.int32, (LANES, LANES), 0) // NA_HEAD_DIM
    c = lax.broadcasted_iota(jnp.int32, (LANES, LANES), 1) // NA_HEAD_DIM
    same_head = jnp.where(r == c, 1.0, 0.0).astype(BF16)
    for src, g_ref, dst, scale in ((q_ref, gq_ref, qo_ref, NA_HEAD_DIM ** -0.5), (k_ref, gk_ref, ko_ref, None)):
        for p in range(NA_WIDTH // LANES):
            sl = slice(p * LANES, (p + 1) * LANES)
            x = src[:, sl].astype(F32)
            sq = x * x
            hi = sq.astype(BF16)
            lo = (sq - hi.astype(F32)).astype(BF16)
            ss = (jnp.dot(hi, same_head, preferred_element_type=F32)
                  + jnp.dot(lo, same_head, preferred_element_type=F32))
            y = x * lax.rsqrt(ss / NA_HEAD_DIM + EPS) * g_ref[...]
            if scale is not None:
                y = y * scale
            dst[:, sl] = y.astype(BF16)


def _qknorm(proj, q_norm_g, k_norm_g):
    t = proj.shape[0]
    tm = QKN_TM
    reps = LANES // NA_HEAD_DIM
    gq = jnp.tile(q_norm_g, reps).reshape(1, LANES)
    gk = jnp.tile(k_norm_g, reps).reshape(1, LANES)
    qcol = COL_QKV // NA_WIDTH
    return pl.pallas_call(
        _qknorm_kernel,
        grid=(t // tm,),
        in_specs=[
            pl.BlockSpec((tm, NA_WIDTH), lambda i: (i, qcol)),
            pl.BlockSpec((tm, NA_WIDTH), lambda i: (i, qcol + 1)),
            pl.BlockSpec((1, LANES), lambda i: (0, 0)),
            pl.BlockSpec((1, LANES), lambda i: (0, 0)),
        ],
        out_specs=[pl.BlockSpec((tm, NA_WIDTH), lambda i: (i, 0))] * 2,
        out_shape=[jax.ShapeDtypeStruct((t, NA_WIDTH), BF16)] * 2,
        compiler_params=_cparams(("parallel",)),
        name="qknorm",
    )(proj, proj, gq, gk)


def _natten_kernel(*refs):
    q_ref = refs[0]
    k_refs = refs[1:1 + WIN_ROWS]
    v_refs = refs[1 + WIN_ROWS:1 + 2 * WIN_ROWS]
    bias_ref = refs[1 + 2 * WIN_ROWS]
    o_ref = refs[2 + 2 * WIN_ROWS]
    first = lax.broadcasted_iota(jnp.int32, (GRID_W, LANES), 1) < NA_HEAD_DIM
    for p in range(NA_WIDTH // LANES):
        sl = slice(p * LANES, (p + 1) * LANES)
        qp = q_ref[:, sl].astype(F32)
        qrows = jnp.concatenate([jnp.where(first, qp, 0.0), jnp.where(first, 0.0, qp)], axis=0).astype(BF16)
        kp = jnp.concatenate([r[:, sl] for r in k_refs], axis=0)
        vp = jnp.concatenate([r[:, sl] for r in v_refs], axis=0)
        st = lax.dot_general(kp, qrows, (((1,), (1,)), ((), ())), preferred_element_type=F32)
        st = st + bias_ref[0, p]
        m = jnp.max(st, axis=0, keepdims=True)
        pe = jnp.exp(st - m)
        inv = 1.0 / jnp.sum(pe, axis=0, keepdims=True)
        pn = (pe * inv).astype(BF16)
        o2 = lax.dot_general(pn, vp, (((0,), (0,)), ((), ())), preferred_element_type=F32)
        o_ref[:, sl] = jnp.where(first, o2[:GRID_W], o2[GRID_W:]).astype(BF16)


def _na_bias_table(rpb):
    cols = jnp.arange(GRID_W)
    c0 = jnp.clip(cols - WIN_COLS // 2, 0, GRID_W - WIN_COLS)
    j = jnp.arange(GRID_W)
    valid = (j[None, :] >= c0[:, None]) & (j[None, :] < c0[:, None] + WIN_COLS)
    k = jnp.arange(2 * WIN_COLS - 1)
    hp = LANES // NA_HEAD_DIM
    x = jnp.arange(LANES)
    onehot = ((x[None, None, None, :] // GRID_W == jnp.arange(hp)[:, None, None, None])
              & (k[None, :, None, None] == j[None, None, :, None] - x[None, None, None, :] % GRID_W + (WIN_COLS - 1))
              ).astype(F32)
    toep = jnp.einsum("pgrk,gkjx->prjx", rpb.astype(F32).reshape(NA_HEADS // hp, hp, *rpb.shape[1:]), onehot,
                      precision=lax.Precision.HIGHEST)
    valid_x = jnp.tile(valid.T, (1, hp))
    toep = jnp.where(valid_x[None, None], toep, -jnp.inf)
    per_delta = jnp.stack([toep[:, WIN_ROWS - 1 - d:2 * WIN_ROWS - 1 - d] for d in range(WIN_ROWS)], axis=0)
    return per_delta.reshape(WIN_ROWS, NA_HEADS // hp, WIN_ROWS * GRID_W, LANES)


def _natten(qn, kn, proj, rpb, nseq, seqlen):
    t = qn.shape[0]
    rows = seqlen // GRID_W
    assert rows >= WIN_ROWS
    r0 = lambda r: jnp.clip(r - WIN_ROWS // 2, 0, rows - WIN_ROWS)
    vcol = COL_QKV // NA_WIDTH + 2
    tbl = _na_bias_table(rpb)

    def win_spec(i, colblk):
        return pl.BlockSpec((GRID_W, NA_WIDTH), lambda s, r: (s * rows + r0(r) + i, colblk))

    return pl.pallas_call(
        _natten_kernel,
        grid=(nseq, rows),
        in_specs=([pl.BlockSpec((GRID_W, NA_WIDTH), lambda s, r: (s * rows + r, 0))]
                  + [win_spec(i, 0) for i in range(WIN_ROWS)]
                  + [win_spec(i, vcol) for i in range(WIN_ROWS)]
                  + [pl.BlockSpec((1,) + tbl.shape[1:], lambda s, r: (r - r0(r), 0, 0, 0))]),
        out_specs=pl.BlockSpec((GRID_W, NA_WIDTH), lambda s, r: (s * rows + r, 0)),
        out_shape=jax.ShapeDtypeStruct((t, NA_WIDTH), BF16),
        compiler_params=_cparams(("parallel", "arbitrary")),
        name="natten",
    )(qn, *([kn] * WIN_ROWS), *([proj] * WIN_ROWS), tbl)


OUTPROJ_TM = 512


def _outproj_kernel(na, yf_ref, yb_ref, xs_ref, z_ref, g1_ref, g2_ref, yna_ref, xa_ref, xb_ref, w1_ref, w2_ref,
                    dsk_ref, ng_ref, gffn_ref, x1_ref, t_ref):
    y = yf_ref[...].astype(F32) + yb_ref[...].astype(F32) + dsk_ref[...] * xs_ref[...].astype(F32)
    y = y * jax.nn.silu(z_ref[...].astype(F32))
    yn = _rms(y, ng_ref[...]).astype(BF16)
    br_ssd = jnp.dot(yn, w1_ref[...], preferred_element_type=F32)
    br_na = jnp.dot(yna_ref[...], w2_ref[...], preferred_element_type=F32)
    mix = (jax.nn.sigmoid(g1_ref[...].astype(F32)) * br_ssd
           + jax.nn.sigmoid(g2_ref[...].astype(F32)) * br_na)
    x1 = _cat_tile(na, xa_ref, xb_ref) + mix
    x1_ref[...] = x1
    t_ref[...] = pltpu.bitcast(_rms(x1, gffn_ref[...]).astype(BF16), jnp.uint32)


def _outproj(yf, yb, xbc, proj, yna, xa, xb, w_out, d_skip, ssd_norm_g, g_ffn):
    t = xa.shape[0] + xb.shape[0]
    tm = OUTPROJ_TM
    na = xa.shape[0] // tm
    tok = lambda w, cb: pl.BlockSpec((tm, w), lambda i: (i, cb))
    row = lambda w: pl.BlockSpec((1, w), lambda i: (0, 0))
    dsk = jnp.repeat(d_skip, SSD_HEAD_DIM).reshape(1, D_INNER)
    return pl.pallas_call(
        functools.partial(_outproj_kernel, na),
        grid=(t // tm,),
        in_specs=[
            tok(D_INNER, 0), tok(D_INNER, 0), tok(D_INNER, 0),
            tok(D_INNER, COL_Z // D_INNER),
            tok(D_MODEL, COL_GATE // D_MODEL), tok(D_MODEL, COL_GATE // D_MODEL + 1),
            tok(NA_WIDTH, 0), *_cat_specs(tm, D_MODEL, na, 1),
            pl.BlockSpec((D_INNER, D_MODEL), lambda i: (0, 0)),
            pl.BlockSpec((NA_WIDTH, D_MODEL), lambda i: (D_INNER // NA_WIDTH, 0)),
            row(D_INNER), row(D_INNER), row(D_MODEL),
        ],
        out_specs=[tok(D_MODEL, 0), pl.BlockSpec((tm // 2, D_MODEL), lambda i: (i, 0))],
        out_shape=[jax.ShapeDtypeStruct((t, D_MODEL), F32), jax.ShapeDtypeStruct((t // 2, D_MODEL), jnp.uint32)],
        compiler_params=_cparams(("parallel",)),
        name="outproj",
    )(yf, yb, xbc, proj, proj, proj, yna, xa, xb, w_out, w_out, dsk,
      ssd_norm_g.reshape(1, D_INNER), g_ffn.reshape(1, D_MODEL))


ROUTE_TR = 256
NEG_INF = float("-inf")
RANK_NONE = 64.0


def _top_values(ss, k, out_refs=None, want_rank=None):
    n = len(ss)
    ss = list(ss)
    out_refs = out_refs or [None] * n
    want_rank = want_rank or [False] * n
    ranks = [jnp.full(s.shape, RANK_NONE, F32) if w else None for s, w in zip(ss, want_rank)]
    mx = [None] * n
    for r in range(k):
        for i in range(n):
            mx[i] = jnp.max(ss[i], axis=0, keepdims=True)
            if out_refs[i] is not None:
                out_refs[i][r:r + 1, :] = mx[i]
            hit = ss[i] == mx[i]
            if ranks[i] is not None:
                ranks[i] = jnp.where(hit, float(r), ranks[i])
            ss[i] = jnp.where(hit, NEG_INF, ss[i])
    return mx, ranks


ROUTE_HEAD_GROUP = 4


def _route_kernel(t_ref, wpq_ref, sk_ref, rank2_ref, b_ref, cnt_ref, a_ref, top1, top2):
    half = PEER_QUERY_DIM // 2
    sub = 8
    tokens = pltpu.bitcast(t_ref[...], BF16)
    row = lax.broadcasted_iota(jnp.int32, (sub, tokens.shape[0]), 0)
    q = jnp.dot(tokens, wpq_ref[...], preferred_element_type=F32).astype(BF16)
    for h0 in range(0, PEER_HEADS, ROUTE_HEAD_GROUP):
        heads = range(h0, h0 + ROUTE_HEAD_GROUP)
        s1, s2 = [], []
        for h in heads:
            for side, dst in ((0, s1), (1, s2)):
                qb = q[:, (2 * h + side) * half:(2 * h + side + 1) * half]
                dst.append(lax.dot_general(sk_ref[side], qb, (((1,), (1,)), ((), ())),
                                           preferred_element_type=F32))
        _, ranks = _top_values(s1 + s2, PEER_TOPK, [top1.at[h] for h in heads] + [top2.at[h] for h in heads],
                               [False] * len(s1) + [True] * len(s2))
        rank2 = ranks[len(s1):]
        cands = []
        for h in heads:
            t1, t2 = top1.at[h], top2.at[h]
            tiles = [t1[0:1, :] + t2[0:sub, :], t1[0:1, :] + t2[sub:, :]]
            for r1 in range(1, sub):
                tile = t1[r1:r1 + 1, :] + t2[0:sub, :]
                tiles.append(jnp.where(row < PEER_TOPK // (r1 + 1), tile, NEG_INF))
            tiles.append(t1[sub:, :] + t2[0:1, :])
            cands.append(jnp.concatenate(tiles, axis=0))
        taus, _ = _top_values(cands, PEER_TOPK)
        for g, h in enumerate(heads):
            t1, t2 = top1.at[h], top2.at[h]
            cand, tau = cands[g], taus[g]
            m1 = t1[0:1, :]
            m2 = t2[0:1, :]
            z = jnp.sum(jnp.where(cand >= tau, jnp.exp(cand - (m1 + m2)), 0.0), axis=0, keepdims=True)
            first = t1[...]
            cnt = jnp.zeros_like(s1[g])
            for r2 in range(PEER_TOPK):
                passing = jnp.where((first + t2[r2:r2 + 1, :]) >= tau, first, jnp.inf)
                cnt = jnp.where(s1[g] >= jnp.min(passing, axis=0, keepdims=True), float(r2 + 1), cnt)
            rank2_ref[h] = pltpu.bitcast(rank2[g].astype(BF16), jnp.uint32)
            b_ref[h] = pltpu.bitcast((jnp.exp(s2[g] - m2) / z).astype(BF16), jnp.uint32)
            cnt_ref[h] = cnt
            a_ref[h] = jnp.exp(s1[g] - m1)


def _route(tn, w_pq, sub_keys):
    t = 2 * tn.shape[0]
    tr = ROUTE_TR
    qd = PEER_HEADS * PEER_QUERY_DIM
    out_spec = pl.BlockSpec((PEER_HEADS, PEER_N_KEYS, tr), lambda i: (0, 0, i))
    key_shape = (PEER_HEADS, PEER_N_KEYS, t)
    pair_spec = pl.BlockSpec((PEER_HEADS, PEER_N_KEYS // 2, tr), lambda i: (0, 0, i))
    pair_shape = (PEER_HEADS, PEER_N_KEYS // 2, t)
    return pl.pallas_call(
        _route_kernel,
        grid=(t // tr,),
        in_specs=[
            pl.BlockSpec((tr // 2, D_MODEL), lambda i: (i, 0)),
            pl.BlockSpec((D_MODEL, qd), lambda i: (0, 0)),
            pl.BlockSpec((2, PEER_N_KEYS, PEER_QUERY_DIM // 2), lambda i: (0, 0, 0)),
        ],
        out_specs=[pair_spec, pair_spec, out_spec, out_spec],
        out_shape=[jax.ShapeDtypeStruct(pair_shape, jnp.uint32), jax.ShapeDtypeStruct(pair_shape, jnp.uint32),
                   jax.ShapeDtypeStruct(key_shape, F32), jax.ShapeDtypeStruct(key_shape, F32)],
        scratch_shapes=[pltpu.VMEM((PEER_HEADS, PEER_TOPK, tr), F32)] * 2,
        compiler_params=_cparams(("parallel",)),
        name="route",
    )(tn, w_pq, sub_keys)


PEER_TP = 512
PEER_I1 = 8
PEER_EB = PEER_I1 * PEER_N_KEYS
NT_DIMS = (((1,), (1,)), ((), ()))
GATE_ROWS = PEER_N_KEYS


def _gate_block(first_keys, st_slot, g_slot, rank2_ref, b_ref, cnt_ref, a_ref):
    rows = GATE_ROWS
    for i in first_keys:
        for part in range(PEER_N_KEYS // rows):
            r0 = part * rows
            e0 = i * PEER_N_KEYS + r0
            for ck in range(st_slot.shape[1] // LANES):
                cols = slice(ck * LANES, (ck + 1) * LANES)
                wgt = jnp.zeros((rows, LANES), BF16)
                for h in range(PEER_HEADS):
                    cnt = cnt_ref[h, i:i + 1, cols].astype(BF16)
                    a1 = a_ref[h, i:i + 1, cols].astype(BF16)
                    rank2 = pltpu.bitcast(rank2_ref[h, r0 // 2:(r0 + rows) // 2, cols], BF16)
                    b2 = pltpu.bitcast(b_ref[h, r0 // 2:(r0 + rows) // 2, cols], BF16)
                    wgt = wgt + jnp.where(rank2 < cnt, b2, jnp.zeros_like(b2)) * a1
                pre = st_slot[e0:e0 + rows, cols].astype(BF16)
                act = 0.5 * pre * (1.0 + lax.erf(pre * (2.0 ** -0.5)))
                g_slot[e0 // 2:(e0 + rows) // 2, cols] = pltpu.bitcast(act * wgt, jnp.uint32)


def _peer_kernel(na, t_ref, u_ref, vt_ref, rank2_ref, b_ref, cnt_lo, a_lo, cnt_hi, a_hi, x1_ref, oa_ref, ob_ref,
                 acc, st0, st1, gated0, gated1):
    jj = pl.program_id(1)
    last = pl.num_programs(1) - 1
    eb = PEER_EB
    st = (st0, st1)
    gated = (gated0, gated1)

    def accumulate(k):
        acc[...] = jnp.dot(pltpu.bitcast(vt_ref[:, k * eb:(k + 1) * eb], BF16),
                           pltpu.bitcast(gated[k][...], BF16), preferred_element_type=F32) + acc[...]

    def gate(k):
        cnt_ref, a_ref = (cnt_lo, a_lo) if k == 0 else (cnt_hi, a_hi)
        _gate_block(range(PEER_I1), st[1 - k], gated[1 - k], rank2_ref, b_ref, cnt_ref, a_ref)

    def scores(k):
        st[k][...] = lax.dot_general(pltpu.bitcast(u_ref[k * eb // 2:(k + 1) * eb // 2, :], BF16),
                                     pltpu.bitcast(t_ref[...], BF16), NT_DIMS, preferred_element_type=F32)

    @pl.when(jj == 0)
    def _():
        acc[...] = jnp.zeros_like(acc)
        scores(0)
        scores(1)
        gate(1)

    @pl.when((jj > 0) & (jj < last))
    def _():
        scores(0)
        gate(0)
        scores(1)
        accumulate(0)
        accumulate(1)
        gate(1)

    @pl.when(jj == last)
    def _():
        gate(0)
        accumulate(0)
        accumulate(1)

    for o_ref, mine in ((oa_ref, pl.program_id(0) < na), (ob_ref, pl.program_id(0) >= na)):
        @pl.when((jj == last) & mine)
        def _(o_ref=o_ref):
            o_ref[...] = x1_ref[...] + acc[...].T


def _peer(tn, u_bf, vt_bf, rank2, b, cnt, a, x1, ta):
    t = x1.shape[0]
    tp, eb = PEER_TP, PEER_EB
    assert ta % tp == 0
    na = ta // tp
    nb = 2 * u_bf.shape[0] // eb
    assert nb % 2 == 0
    pair_spec = pl.BlockSpec((PEER_HEADS, PEER_N_KEYS // 2, tp), lambda i, j: (0, 0, i))
    lo_spec = pl.BlockSpec((PEER_HEADS, PEER_I1, tp), lambda i, j: (0, jnp.maximum(2 * j - 1, 0), i))
    hi_spec = pl.BlockSpec((PEER_HEADS, PEER_I1, tp), lambda i, j: (0, jnp.minimum(2 * j, nb - 1), i))
    return pl.pallas_call(
        functools.partial(_peer_kernel, na),
        grid=(t // tp, nb // 2 + 1),
        in_specs=[
            pl.BlockSpec((tp // 2, D_MODEL), lambda i, j: (i, 0)),
            pl.BlockSpec((eb, D_MODEL), lambda i, j: (jnp.minimum(j, nb // 2 - 1), 0)),
            pl.BlockSpec((D_MODEL // 2, 2 * eb), lambda i, j: (0, jnp.maximum(j - 1, 0))),
            pair_spec, pair_spec, lo_spec, lo_spec, hi_spec, hi_spec,
            pl.BlockSpec((tp, D_MODEL), lambda i, j: (i, 0)),
        ],
        out_specs=_cat_specs(tp, D_MODEL, na, 2),
        out_shape=[jax.ShapeDtypeStruct((ta, D_MODEL), F32), jax.ShapeDtypeStruct((t - ta, D_MODEL), F32)],
        scratch_shapes=[pltpu.VMEM((D_MODEL, tp), F32), pltpu.VMEM((eb, tp), F32), pltpu.VMEM((eb, tp), F32),
                        pltpu.VMEM((eb // 2, tp), jnp.uint32), pltpu.VMEM((eb // 2, tp), jnp.uint32)],
        compiler_params=_cparams(("arbitrary", "arbitrary")),
        name="peer",
    )(tn, u_bf, vt_bf, rank2, b, cnt, a, cnt, a, x1)


PACK_TILE = 1024


def _pack_kernel(transpose, w_ref, o_ref):
    w = w_ref[...]
    if transpose:
        w = w.T
    o_ref[...] = pltpu.bitcast(w.astype(BF16), jnp.uint32)


def _pack_table(w, transpose):
    n, c = w.shape
    tile = PACK_TILE
    assert n % tile == 0 and c == tile
    if transpose:
        out_shape, out_spec = (c // 2, n), pl.BlockSpec((c // 2, tile), lambda i: (0, i))
    else:
        out_shape, out_spec = (n // 2, c), pl.BlockSpec((tile // 2, c), lambda i: (i, 0))
    return pl.pallas_call(
        functools.partial(_pack_kernel, transpose),
        grid=(n // tile,),
        in_specs=[pl.BlockSpec((tile, c), lambda i: (i, 0))],
        out_specs=out_spec,
        out_shape=jax.ShapeDtypeStruct(out_shape, jnp.uint32),
        compiler_params=_cparams(("parallel",)),
        name="pack_t" if transpose else "pack",
    )(w)


def _regroup_w_in(w_in):
    pad = jnp.zeros((D_MODEL, PROJ_COLS - COL_DT - 2 * SSD_HEADS), w_in.dtype)
    return jnp.concatenate([w_in[:, OFF_XBC:OFF_DT], w_in[:, OFF_QKV:OFF_GATE], w_in[:, :OFF_XBC],
                            w_in[:, OFF_GATE:], w_in[:, OFF_DT:OFF_QKV], pad], axis=1).astype(BF16)


def _encoder_layer(xa, xb, seqlen, g_mix, w_in, conv_w, conv_b, dt_bias, a_log, d_skip, ssd_norm_g,
                   q_norm_g, k_norm_g, rpb, w_out, g_ffn, w_pq, sub_keys, expert_u, expert_v):
    nseq = (xa.shape[0] + xb.shape[0]) // seqlen
    proj, dt_raw = _inproj(xa, xb, g_mix, _regroup_w_in(w_in))
    xbc = _conv(proj, conv_w, conv_b, nseq, seqlen)
    yf, yb = _ssd(xbc, dt_raw, dt_bias, a_log, nseq, seqlen)
    qn, kn = _qknorm(proj, q_norm_g, k_norm_g)
    yna = _natten(qn, kn, proj, rpb, nseq, seqlen)
    x1, tn = _outproj(yf, yb, xbc, proj, yna, xa, xb, w_out.astype(BF16), d_skip, ssd_norm_g, g_ffn)
    rank2, b, cnt, a = _route(tn, w_pq.astype(BF16), sub_keys.astype(BF16))
    return _peer(tn, _pack_table(expert_u, False), _pack_table(expert_v, True), rank2, b, cnt, a, x1,
                 xa.shape[0])


def kernel(x_prompt, x_sample, g_mix, w_in, conv_w, conv_b, dt_bias, a_log, d_skip, ssd_norm_g, q_norm_g,
           k_norm_g, rpb, w_out, g_ffn, w_pq, sub_keys, expert_u, expert_v):
    assert x_prompt.shape[1:] == x_sample.shape[1:]
    seqlen = x_prompt.shape[1]
    xa = x_prompt.reshape(-1, D_MODEL)
    xb = x_sample.reshape(-1, D_MODEL)
    for i in range(g_mix.shape[0]):
        xa, xb = _encoder_layer(xa, xb, seqlen, g_mix[i], w_in[i], conv_w[i], conv_b[i], dt_bias[i], a_log[i],
                                d_skip[i], ssd_norm_g[i], q_norm_g[i], k_norm_g[i], rpb[i], w_out[i], g_ffn[i],
                                w_pq[i], sub_keys[i], expert_u[i], expert_v[i])
    return (xa.reshape(x_prompt.shape), xb.reshape(x_sample.shape))
```
